```python
import jax
import jax.numpy as jnp
from jax import lax
import numpy as np

D_MODEL = 1024
BATCH = 8
SEQ = 2048
DEPTH = 4

GRID_W = 64
CTX_LEN = 256
N_EVEN = (DEPTH + 1) // 2
N_ODD = DEPTH // 2
MIX_W = D_MODEL
CHUNK = 128
A_GROUPS = 8
A_GROUP_DIM = MIX_W // A_GROUPS
CONV_W = 31
N_HEADS = 16
N_KV_HEADS = 4
HEAD_DIM = D_MODEL // N_HEADS
KV_GROUP = N_HEADS // N_KV_HEADS
WINDOW = 128
BLOCK = 128
ROPE_THETA = 10000.0
ROPE_AXIS_DIM = HEAD_DIM // 2
D_FF = 3584
N_EXPERTS = 8
TOP_K = 2
EPS = 1e-6
NEG_INF = -1e30

kernel_name = 'hybrid_dit_gmlp_conformer_swa_moe'


def rms_norm(x, g):
    xf = x.astype(jnp.float32)
    y = xf * lax.rsqrt(jnp.mean(xf * xf, axis=-1, keepdims=True) + EPS)
    return (y * g.astype(jnp.float32)).astype(x.dtype)


def layer_norm(x, g, b):
    xf = x.astype(jnp.float32)
    mu = jnp.mean(xf, axis=-1, keepdims=True)
    xc = xf - mu
    y = xc * lax.rsqrt(jnp.mean(xc * xc, axis=-1, keepdims=True) + EPS)
    return (y * g.astype(jnp.float32) + b.astype(jnp.float32)).astype(x.dtype)


def ada_mod(cvec, w, b):
    m = jax.nn.silu(cvec) @ w + b
    m = m.reshape(-1, 1, 6 * D_MODEL)
    return jnp.split(m, 6, axis=-1)


def rope_1d(x, ang):
    x1, x2 = jnp.split(x, 2, axis=-1)
    shp = (1, ang.shape[0]) + (1,) * (x.ndim - 3) + (ang.shape[1],)
    cos = jnp.cos(ang).reshape(shp).astype(x.dtype)
    sin = jnp.sin(ang).reshape(shp).astype(x.dtype)
    return jnp.concatenate([x1 * cos - x2 * sin, x2 * cos + x1 * sin], axis=-1)


def rope_2d(x, ang_row, ang_col):
    xr, xc = jnp.split(x, 2, axis=-1)
    return jnp.concatenate([rope_1d(xr, ang_row), rope_1d(xc, ang_col)], axis=-1)


def even_mixer(h, w_in, ln_g, ln_b, ws, bs, conv_w, conv_b, cn_g, w_out):
    bsz, length, _ = h.shape
    z = h @ w_in
    za, zb = z[..., :2 * MIX_W], z[..., 2 * MIX_W:]
    za = jax.nn.gelu(za)
    u, v = jnp.split(za, 2, axis=-1)
    v = layer_norm(v, ln_g, ln_b)
    v = v.reshape(bsz, length // CHUNK, CHUNK, A_GROUPS, A_GROUP_DIM)
    v = jnp.einsum('gpq,bnqgc->bnpgc', ws, v) + bs.T[None, None, :, :, None]
    y_a = u * v.reshape(bsz, length, MIX_W)
    a, gt = jnp.split(zb, 2, axis=-1)
    g = a * jax.nn.sigmoid(gt)
    g = lax.conv_general_dilated(
        g, conv_w[:, None, :], window_strides=(1,),
        padding=[(CONV_W // 2, CONV_W // 2)],
        dimension_numbers=('NWC', 'WIO', 'NWC'),
        feature_group_count=MIX_W) + conv_b
    y_b = jax.nn.silu(rms_norm(g, cn_g))
    return jnp.concatenate([y_a, y_b], axis=-1) @ w_out


def sink_softmax(scores, sink_g):
    lead = scores[0].shape[:-1]
    sink_col = jnp.broadcast_to(sink_g[None, :, :, None, None], lead + (1,))
    p = jax.nn.softmax(jnp.concatenate(list(scores) + [sink_col], axis=-1), axis=-1)
    cuts, acc = [], 0
    for s in scores[:-1]:
        acc += s.shape[-1]
        cuts.append(acc)
    return jnp.split(p[..., :-1], cuts, axis=-1)


def window_attention(hl, hc, w_qkv, q_g, k_g, sink, w_o, ang_row, ang_col, need_ctx):
    bsz, s_len, _ = hl.shape
    c_len = hc.shape[1]
    n_q = N_HEADS * HEAD_DIM
    n_kv = N_KV_HEADS * HEAD_DIM
    scale = HEAD_DIM ** -0.5
    sink_g = sink.astype(jnp.float32).reshape(N_KV_HEADS, KV_GROUP)

    def split_q(z, length):
        return rms_norm(z[..., :n_q].reshape(bsz, length, N_KV_HEADS, KV_GROUP, HEAD_DIM), q_g)

    def split_kv(z, length):
        k = rms_norm(z[..., n_q:n_q + n_kv].reshape(bsz, length, N_KV_HEADS, HEAD_DIM), k_g)
        v = z[..., n_q + n_kv:].reshape(bsz, length, N_KV_HEADS, HEAD_DIM)
        return k, v

    zl = hl @ w_qkv
    zc = hc @ w_qkv
    ql = rope_2d(split_q(zl, s_len), ang_row, ang_col)
    kl, vl = split_kv(zl, s_len)
    kl = rope_2d(kl, ang_row, ang_col)
    kc, vc = split_kv(zc, c_len)

    kpad = jnp.pad(kl, ((0, 0), (BLOCK, BLOCK), (0, 0), (0, 0)))
    vpad = jnp.pad(vl, ((0, 0), (BLOCK, BLOCK), (0, 0), (0, 0)))
    n_blocks = s_len // BLOCK

    def block(i):
        q0 = i * BLOCK
        qb = lax.dynamic_slice_in_dim(ql, q0, BLOCK, axis=1)
        kb = lax.dynamic_slice_in_dim(kpad, q0, 3 * BLOCK, axis=1)
        vb = lax.dynamic_slice_in_dim(vpad, q0, 3 * BLOCK, axis=1)
        s_win = jnp.einsum('bqkgd,bskd->bkgqs', qb, kb).astype(jnp.float32) * scale
        qpos = q0 + jnp.arange(BLOCK)
        kpos = q0 - BLOCK + jnp.arange(3 * BLOCK)
        valid = (kpos[None, :] >= 0) & (kpos[None, :] < s_len) & (jnp.abs(qpos[:, None] - kpos[None, :]) <= WINDOW)
        s_win = jnp.where(valid, s_win, NEG_INF)
        s_ctx = jnp.einsum('bqkgd,bskd->bkgqs', qb, kc).astype(jnp.float32) * scale
        p_win, p_ctx = sink_softmax([s_win, s_ctx], sink_g)
        o = (jnp.einsum('bkgqs,bskd->bqkgd', p_win.astype(vb.dtype), vb)
             + jnp.einsum('bkgqs,bskd->bqkgd', p_ctx.astype(vc.dtype), vc))
        return o.reshape(bsz, BLOCK, n_q)

    o_lat = lax.map(block, jnp.arange(n_blocks))
    y_lat = jnp.moveaxis(o_lat, 0, 1).reshape(bsz, s_len, n_q) @ w_o

    y_ctx = None
    if need_ctx:
        qc = split_q(zc, c_len)
        s_c = jnp.einsum('bqkgd,bskd->bkgqs', qc, kc).astype(jnp.float32) * scale
        (p_c,) = sink_softmax([s_c], sink_g)
        o_c = jnp.einsum('bkgqs,bskd->bqkgd', p_c.astype(vc.dtype), vc)
        y_ctx = o_c.reshape(bsz, c_len, n_q) @ w_o
    return y_lat, y_ctx


def swiglu(h, w1, w3, w2):
    return (jax.nn.silu(h @ w1) * (h @ w3)) @ w2


def moe_swiglu(h, router, w1, w3, w2):
    logits = (h @ router).astype(jnp.float32)
    vals, idx = lax.top_k(logits, TOP_K)
    wts = jax.nn.softmax(vals, axis=-1)
    combine = jnp.sum(jax.nn.one_hot(idx, N_EXPERTS, dtype=jnp.float32) * wts[..., None], axis=-2).astype(h.dtype)
    y = jnp.zeros_like(h)
    for e in range(N_EXPERTS):
        y = y + combine[..., e:e + 1] * swiglu(h, w1[e], w3[e], w2[e])
    return y


def _normal(key, shape, scale):
    return jax.random.normal(key, shape, dtype=jnp.float32) * scale


def setup_inputs(seed: int = 0) -> dict:
    key = jax.random.key(seed)
    ks = jax.random.split(key, 29)
    D = D_MODEL
    return {
        'x': _normal(ks[0], (BATCH, SEQ, D), 1.0),
        'c': _normal(ks[1], (BATCH, D), 1.0),
        'ctx': _normal(ks[2], (BATCH, CTX_LEN, D), 1.0),
        'c_ctx': _normal(ks[3], (D,), 1.0),
        'ada_w': _normal(ks[4], (DEPTH, D, 6 * D), 0.5 * D ** -0.5),
        'ada_b': _normal(ks[5], (DEPTH, 6 * D), 0.02),
        'norm_mix_g': 1.0 + _normal(ks[6], (DEPTH, D), 0.05),
        'norm_ffn_g': 1.0 + _normal(ks[7], (DEPTH, D), 0.05),
        'ev_w_in': _normal(ks[8], (N_EVEN, D, 4 * MIX_W), D ** -0.5),
        'ev_ln_g': 1.0 + _normal(ks[9], (N_EVEN, MIX_W), 0.05),
        'ev_ln_b': _normal(ks[10], (N_EVEN, MIX_W), 0.02),
        'ev_ws': _normal(ks[11], (N_EVEN, A_GROUPS, CHUNK, CHUNK), CHUNK ** -0.5),
        'ev_bs': 1.0 + _normal(ks[12], (N_EVEN, A_GROUPS, CHUNK), 0.05),
        'ev_conv_w': _normal(ks[13], (N_EVEN, CONV_W, MIX_W), CONV_W ** -0.5),
        'ev_conv_b': _normal(ks[14], (N_EVEN, MIX_W), 0.02),
        'ev_cnorm_g': 1.0 + _normal(ks[15], (N_EVEN, MIX_W), 0.05),
        'ev_w_out': _normal(ks[16], (N_EVEN, 2 * MIX_W, D), (2 * MIX_W) ** -0.5),
        'od_w_qkv': _normal(ks[17], (N_ODD, D, (N_HEADS + 2 * N_KV_HEADS) * HEAD_DIM), D ** -0.5),
        'od_q_g': 1.0 + _normal(ks[18], (N_ODD, HEAD_DIM), 0.05),
        'od_k_g': 1.0 + _normal(ks[19], (N_ODD, HEAD_DIM), 0.05),
        'od_sink': _normal(ks[20], (N_ODD, N_HEADS), 0.5),
        'od_w_o': _normal(ks[21], (N_ODD, N_HEADS * HEAD_DIM, D), (N_HEADS * HEAD_DIM) ** -0.5),
        'ff_w1': _normal(ks[22], (N_EVEN, D, D_FF), D ** -0.5),
        'ff_w3': _normal(ks[23], (N_EVEN, D, D_FF), D ** -0.5),
        'ff_w2': _normal(ks[24], (N_EVEN, D_FF, D), D_FF ** -0.5),
        'moe_router': _normal(ks[25], (N_ODD, D, N_EXPERTS), D ** -0.5),
        'moe_w1': _normal(ks[26], (N_ODD, N_EXPERTS, D, D_FF), D ** -0.5),
        'moe_w3': _normal(ks[27], (N_ODD, N_EXPERTS, D, D_FF), D ** -0.5),
        'moe_w2': _normal(ks[28], (N_ODD, N_EXPERTS, D_FF, D), D_FF ** -0.5),
    }


def reference(x, c, ctx, c_ctx, ada_w, ada_b, norm_mix_g, norm_ffn_g,
              ev_w_in, ev_ln_g, ev_ln_b, ev_ws, ev_bs, ev_conv_w, ev_conv_b, ev_cnorm_g, ev_w_out,
              od_w_qkv, od_q_g, od_k_g, od_sink, od_w_o,
              ff_w1, ff_w3, ff_w2, moe_router, moe_w1, moe_w3, moe_w2):
    s_len = x.shape[1]
    ROWS = s_len // GRID_W
    pos_row = jnp.repeat(jnp.arange(ROWS), GRID_W).astype(jnp.float32)
    pos_col = jnp.tile(jnp.arange(GRID_W), ROWS).astype(jnp.float32)
    inv_freq = ROPE_THETA ** (-jnp.arange(0, ROPE_AXIS_DIM, 2, dtype=jnp.float32) / ROPE_AXIS_DIM)
    ang_row = pos_row[:, None] * inv_freq[None, :]
    ang_col = pos_col[:, None] * inv_freq[None, :]

    xl, xc = x, ctx
    for li in range(DEPTH):
        need_ctx = li < DEPTH - 1
        j = li // 2
        sh1, sc1, g1, sh2, sc2, g2 = ada_mod(c, ada_w[li], ada_b[li])
        csh1, csc1, cg1, csh2, csc2, cg2 = ada_mod(c_ctx, ada_w[li], ada_b[li])

        hl = rms_norm(xl, norm_mix_g[li]) * (1 + sc1) + sh1
        hc = rms_norm(xc, norm_mix_g[li]) * (1 + csc1) + csh1
        if li % 2 == 0:
            p = (ev_w_in[j], ev_ln_g[j], ev_ln_b[j], ev_ws[j], ev_bs[j],
                 ev_conv_w[j], ev_conv_b[j], ev_cnorm_g[j], ev_w_out[j])
            yl = even_mixer(hl, *p)
            yc = even_mixer(hc, *p) if need_ctx else None
        else:
            yl, yc = window_attention(hl, hc, od_w_qkv[j], od_q_g[j], od_k_g[j], od_sink[j], od_w_o[j],
                                      ang_row, ang_col, need_ctx)
        xl = xl + g1 * yl
        if need_ctx:
            xc = xc + cg1 * yc

        hl = rms_norm(xl, norm_ffn_g[li]) * (1 + sc2) + sh2
        if li % 2 == 0:
            xl = xl + g2 * swiglu(hl, ff_w1[j], ff_w3[j], ff_w2[j])
        else:
            xl = xl + g2 * moe_swiglu(hl, moe_router[j], moe_w1[j], moe_w3[j], moe_w2[j])
        if need_ctx:
            hc = rms_norm(xc, norm_ffn_g[li]) * (1 + csc2) + csh2
            if li % 2 == 0:
                xc = xc + cg2 * swiglu(hc, ff_w1[j], ff_w3[j], ff_w2[j])
            else:
                xc = xc + cg2 * moe_swiglu(hc, moe_router[j], moe_w1[j], moe_w3[j], moe_w2[j])
    return xl
```

```python
import functools

import jax
import jax.numpy as jnp
from jax import lax
from jax.experimental import pallas as pl
from jax.experimental.pallas import tpu as pltpu

F32 = jnp.float32
BF16 = jnp.bfloat16
I32 = jnp.int32

D = 1024
NB = 8
SEQ = 2048
CTX_LEN = 256
DEPTH = 4
GRID_W = 64
CHUNK = 128
A_GROUPS = 8
CONV_W = 31
N_HEADS = 16
N_KV = 4
HEAD_DIM = 64
WINDOW = 128
ROPE_THETA = 10000.0
D_FF = 3584
N_EXPERTS = 8
EPS = 1e-6
NEG_INF = -1e30

N_LAT = NB * SEQ
N_CTX = NB * CTX_LEN
N_ALL = N_LAT + N_CTX
CTX_MOD_ROW = NB

LANES = 128
SUBLANES = 8
BF16_ROWS = 16

T_EVEN = 256
HALO = BF16_ROWS
T_FFN = 1024
T_FF = 512
T_ROUTE = 512
T_MOE = 512
T_ROW = 256
T_QKV = 256
T_Q = 128
T_PROJ = 512

VMEM_BIG = 52 * 1024 * 1024


def _mod_spec(tile):
    per = SEQ // tile
    return pl.BlockSpec((1, 1, D), lambda i, *_: (jnp.minimum(i // per, CTX_MOD_ROW), 0, 0))


def _const_spec(shape):
    nd = len(shape)
    return pl.BlockSpec(shape, lambda *_: (0,) * nd)


def _modnorm(x, g, sc, sh):
    ms = jnp.mean(x * x, axis=-1, keepdims=True)
    return (x * lax.rsqrt(ms + EPS) * g) * (1.0 + sc) + sh


def _silu(x):
    return x * jax.nn.sigmoid(x)


def _ada_body(c_ref, w_ref, b_ref, o_ref):
    a = _silu(c_ref[...])
    o_ref[0] = jnp.dot(a.astype(BF16), w_ref[0].astype(BF16), preferred_element_type=F32) + b_ref[0]


def _ada_mods(c, c_ctx, ada_w, ada_b):
    rows = 16
    cc = jnp.concatenate([c, c_ctx[None, :], jnp.zeros((rows - NB - 1, D), F32)], axis=0)
    out = pl.pallas_call(
        _ada_body,
        grid=(DEPTH, 6),
        in_specs=[
            pl.BlockSpec((rows, D), lambda l, n: (0, 0)),
            pl.BlockSpec((1, D, D), lambda l, n: (l, 0, n)),
            pl.BlockSpec((1, 1, D), lambda l, n: (l, 0, n)),
        ],
        out_specs=pl.BlockSpec((1, rows, D), lambda l, n: (l, 0, n)),
        out_shape=jax.ShapeDtypeStruct((DEPTH, rows, 6 * D), F32),
        name="ada_mod",
    )(cc, ada_w, ada_b.reshape(DEPTH, 1, 6 * D))
    m = out[:, :NB + 1].reshape(DEPTH, NB + 1, 6, 1, D)
    return [[m[l, :, k] for k in range(6)] for l in range(DEPTH)]


def _even_body(x_ref, xp_ref, xn_ref, sh_ref, sc_ref, gate_ref, ng_ref, win_ref, lng_ref, lnb_ref,
               ws_ref, bs_ref, cw_ref, cb_ref, cng_ref, wout_ref, o_ref, gext_ref, cv_ref):
    i = pl.program_id(0)
    per_seq = SEQ // T_EVEN
    is_lat = i < N_LAT // T_EVEN
    pos = i % per_seq
    is_start = jnp.logical_or(jnp.logical_not(is_lat), pos == 0)
    is_end = jnp.logical_or(jnp.logical_not(is_lat), pos == per_seq - 1)
    ext = T_EVEN + 2 * HALO

    xm = x_ref[...]
    xt = jnp.concatenate([xp_ref[...], xm, xn_ref[...]], axis=0)
    h = _modnorm(xt, ng_ref[...], sc_ref[0], sh_ref[0]).astype(BF16)

    zb = jnp.dot(h, win_ref[:, 2 * D:], preferred_element_type=F32)
    gg = zb[:, :D] * jax.nn.sigmoid(zb[:, D:])
    row = lax.broadcasted_iota(I32, (ext, 1), 0)
    lo = jnp.where(is_start, HALO, 0)
    hi = jnp.where(is_end, T_EVEN + HALO, ext)
    gg = jnp.where(jnp.logical_and(row >= lo, row < hi), gg, 0.0)
    for cbk in range(D // LANES):
        gext_ref[cbk] = gg[:, cbk * LANES:(cbk + 1) * LANES]

    rows_per = 64

    def conv_block(cbk, carry):
        for rb in range(T_EVEN // rows_per):
            acc = jnp.zeros((rows_per, LANES), F32)
            for k in range(CONV_W):
                off = rb * rows_per + k + HALO - CONV_W // 2
                acc = acc + cw_ref[cbk, pl.ds(k, 1), :] * gext_ref[cbk, pl.ds(off, rows_per), :]
            cv_ref[cbk, pl.ds(rb * rows_per, rows_per), :] = acc
        return carry

    lax.fori_loop(0, D // LANES, conv_block, 0)
    cv = jnp.concatenate([cv_ref[cbk] for cbk in range(D // LANES)], axis=1) + cb_ref[...]
    ms = jnp.mean(cv * cv, axis=-1, keepdims=True)
    yb = _silu(cv * lax.rsqrt(ms + EPS) * cng_ref[...])

    za = jnp.dot(h[HALO:HALO + T_EVEN], win_ref[:, :2 * D], preferred_element_type=F32)
    u = jax.nn.gelu(za[:, :D])
    v = jax.nn.gelu(za[:, D:])
    mu = jnp.mean(v, axis=-1, keepdims=True)
    vc = v - mu
    var = jnp.mean(vc * vc, axis=-1, keepdims=True)
    vn = (vc * lax.rsqrt(var + EPS) * lng_ref[...] + lnb_ref[...]).astype(BF16)
    gdim = D // A_GROUPS
    chunks = []
    for ck in range(T_EVEN // CHUNK):
        blocks = [
            jnp.dot(ws_ref[g], vn[ck * CHUNK:(ck + 1) * CHUNK, g * gdim:(g + 1) * gdim],
                    preferred_element_type=F32)
            for g in range(A_GROUPS)
        ]
        chunks.append(jnp.concatenate(blocks, axis=1) + bs_ref[...])
    ya = u * jnp.concatenate(chunks, axis=0)

    y = (jnp.dot(ya.astype(BF16), wout_ref[:D], preferred_element_type=F32)
         + jnp.dot(yb.astype(BF16), wout_ref[D:], preferred_element_type=F32))
    o_ref[...] = xm + gate_ref[0] * y


def _even_mixer(x, sh, sc, gate, ng, w_in, ln_g, ln_b, ws, bs, conv_w, conv_b, cn_g, w_out):
    n = x.shape[0]
    nt = n // T_EVEN
    hb = T_EVEN // HALO
    last = n // HALO - 1
    ncb = D // LANES
    bs_full = jnp.repeat(bs.T, D // A_GROUPS, axis=1)
    cw = jnp.pad(conv_w, ((0, 32 - CONV_W), (0, 0))).reshape(32, ncb, LANES).transpose(1, 0, 2)
    mod = _mod_spec(T_EVEN)
    row1 = lambda a: a.reshape(1, D)
    return pl.pallas_call(
        _even_body,
        grid=(nt,),
        in_specs=[
            pl.BlockSpec((T_EVEN, D), lambda i: (i, 0)),
            pl.BlockSpec((HALO, D), lambda i: (jnp.maximum(i * hb - 1, 0), 0)),
            pl.BlockSpec((HALO, D), lambda i: (jnp.minimum((i + 1) * hb, last), 0)),
            mod, mod, mod,
            _const_spec((1, D)),
            _const_spec((D, 4 * D)),
            _const_spec((1, D)), _const_spec((1, D)),
            _const_spec((A_GROUPS, CHUNK, CHUNK)),
            _const_spec((CHUNK, D)),
            _const_spec((ncb, 32, LANES)),
            _const_spec((1, D)), _const_spec((1, D)),
            _const_spec((2 * D, D)),
        ],
        out_specs=pl.BlockSpec((T_EVEN, D), lambda i: (i, 0)),
        out_shape=jax.ShapeDtypeStruct((n, D), F32),
        scratch_shapes=[
            pltpu.VMEM((ncb, T_EVEN + 2 * HALO, LANES), F32),
            pltpu.VMEM((ncb, T_EVEN, LANES), F32),
        ],
        compiler_params=pltpu.CompilerParams(
            dimension_semantics=("arbitrary",), vmem_limit_bytes=VMEM_BIG),
        name="even_mixer",
    )(x, x, x, sh, sc, gate, row1(ng), w_in.astype(BF16), row1(ln_g), row1(ln_b), ws.astype(BF16),
      bs_full, cw, row1(conv_b), row1(cn_g), w_out.astype(BF16))


def _ffn_dense_body(x_ref, sh_ref, sc_ref, gate_ref, ng_ref, w1_ref, w3_ref, w2_ref, o_ref,
                    h_ref, acc_ref):
    j = pl.program_id(1)

    @pl.when(j == 0)
    def _():
        h_ref[...] = _modnorm(x_ref[...], ng_ref[...], sc_ref[0], sh_ref[0]).astype(BF16)
        acc_ref[...] = jnp.zeros_like(acc_ref)

    h = h_ref[...]
    a = jnp.dot(h, w1_ref[...], preferred_element_type=F32)
    b = jnp.dot(h, w3_ref[...], preferred_element_type=F32)
    t = (_silu(a) * b).astype(BF16)
    acc_ref[...] += jnp.dot(t, w2_ref[...], preferred_element_type=F32)

    @pl.when(j == pl.num_programs(1) - 1)
    def _():
        o_ref[...] = x_ref[...] + gate_ref[0] * acc_ref[...]


def _ffn_dense(x, sh, sc, gate, ng, w1, w3, w2):
    n = x.shape[0]
    mod = _mod_spec(T_FFN)
    return pl.pallas_call(
        _ffn_dense_body,
        grid=(n // T_FFN, D_FF // T_FF),
        in_specs=[
            pl.BlockSpec((T_FFN, D), lambda i, j: (i, 0)),
            mod, mod, mod,
            _const_spec((1, D)),
            pl.BlockSpec((D, T_FF), lambda i, j: (0, j)),
            pl.BlockSpec((D, T_FF), lambda i, j: (0, j)),
            pl.BlockSpec((T_FF, D), lambda i, j: (j, 0)),
        ],
        out_specs=pl.BlockSpec((T_FFN, D), lambda i, j: (i, 0)),
        out_shape=jax.ShapeDtypeStruct((n, D), F32),
        scratch_shapes=[pltpu.VMEM((T_FFN, D), BF16), pltpu.VMEM((T_FFN, D), F32)],
        compiler_params=pltpu.CompilerParams(
            dimension_semantics=("arbitrary", "arbitrary"), vmem_limit_bytes=VMEM_BIG),
        name="ffn_dense",
    )(x, sh, sc, gate, ng.reshape(1, D), w1.astype(BF16), w3.astype(BF16), w2.astype(BF16))


def _router_body(x_ref, sh_ref, sc_ref, ng_ref, rt_ref, h_ref, i1_ref, i2_ref, w1_ref, w2_ref,
                 r1_ref, r2_ref, cnt_ref, carry_ref):
    i = pl.program_id(0)

    @pl.when(i == 0)
    def _():
        carry_ref[...] = jnp.zeros_like(carry_ref)

    h = _modnorm(x_ref[...], ng_ref[...], sc_ref[0], sh_ref[0])
    h_ref[...] = h
    lg = lax.dot_general(rt_ref[...], h, (((1,), (1,)), ((), ())),
                         precision=lax.Precision.HIGHEST, preferred_element_type=F32)
    e = lax.broadcasted_iota(I32, lg.shape, 0).astype(F32)
    big = float(N_EXPERTS)
    m1 = jnp.max(lg, axis=0, keepdims=True)
    i1 = jnp.min(jnp.where(lg == m1, e, big), axis=0, keepdims=True)
    lg2 = jnp.where(e == i1, -jnp.inf, lg)
    m2 = jnp.max(lg2, axis=0, keepdims=True)
    i2 = jnp.min(jnp.where(lg2 == m2, e, big), axis=0, keepdims=True)
    e2 = jnp.exp(m2 - m1)
    den = 1.0 + e2
    w1_ref[0] = 1.0 / den
    w2_ref[0] = e2 / den
    i1_ref[0] = i1.astype(I32)
    i2_ref[0] = i2.astype(I32)

    sel1 = e == i1
    sel2 = e == i2
    member = jnp.where(jnp.logical_or(sel1, sel2), 1.0, 0.0)
    t = lg.shape[1]
    before = (lax.broadcasted_iota(I32, (t, t), 0) < lax.broadcasted_iota(I32, (t, t), 1))
    tri = jnp.where(before, 1.0, 0.0).astype(BF16)
    cum = jnp.dot(member.astype(BF16), tri, preferred_element_type=F32) + carry_ref[:, 0:1]
    r1_ref[0] = jnp.sum(jnp.where(sel1, cum, 0.0), axis=0, keepdims=True).astype(I32)
    r2_ref[0] = jnp.sum(jnp.where(sel2, cum, 0.0), axis=0, keepdims=True).astype(I32)
    carry_ref[...] = carry_ref[...] + jnp.sum(member, axis=1, keepdims=True)
    cnt_ref[...] = carry_ref[...]


def _router(x, n, sh, sc, ng, router):
    nt = n // T_ROUTE
    mod = _mod_spec(T_ROUTE)
    vec_spec = pl.BlockSpec((1, 1, T_ROUTE), lambda i: (i, 0, 0))
    vec_i = jax.ShapeDtypeStruct((nt, 1, T_ROUTE), I32)
    vec_f = jax.ShapeDtypeStruct((nt, 1, T_ROUTE), F32)
    return pl.pallas_call(
        _router_body,
        grid=(nt,),
        in_specs=[
            pl.BlockSpec((T_ROUTE, D), lambda i: (i, 0)),
            mod, mod,
            _const_spec((1, D)),
            _const_spec((N_EXPERTS, D)),
        ],
        out_specs=[pl.BlockSpec((T_ROUTE, D), lambda i: (i, 0)),
                   vec_spec, vec_spec, vec_spec, vec_spec, vec_spec, vec_spec,
                   _const_spec((N_EXPERTS, LANES))],
        out_shape=[jax.ShapeDtypeStruct((n, D), F32), vec_i, vec_i, vec_f, vec_f, vec_i, vec_i,
                   jax.ShapeDtypeStruct((N_EXPERTS, LANES), F32)],
        scratch_shapes=[pltpu.VMEM((N_EXPERTS, LANES), F32)],
        compiler_params=pltpu.CompilerParams(dimension_semantics=("arbitrary",)),
        name="moe_router",
    )(x, sh, sc, ng.reshape(1, D), router.T)


def _row_copies(src_ref, dst_ref, idx_ref, sem, gather):
    def body(r, carry):
        far = idx_ref[0, 0, r]
        if gather:
            pltpu.make_async_copy(src_ref.at[pl.ds(far, 1)], dst_ref.at[pl.ds(r, 1)], sem).start()
        else:
            pltpu.make_async_copy(src_ref.at[pl.ds(r, 1)], dst_ref.at[pl.ds(far, 1)], sem).start()
        return carry

    lax.fori_loop(0, T_ROW, body, 0)


def _wait_rows(src_ref, dst_ref, sem, gather):
    if gather:
        pltpu.make_async_copy(src_ref.at[pl.ds(0, T_ROW)], dst_ref, sem).wait()
    else:
        pltpu.make_async_copy(src_ref, dst_ref.at[pl.ds(0, T_ROW)], sem).wait()


def _dispatch_body(d1_ref, d2_ref, h_ref, xs_in_ref, xs_ref, sem):
    del xs_in_ref
    _row_copies(h_ref, xs_ref, d1_ref, sem.at[0], gather=False)
    _row_copies(h_ref, xs_ref, d2_ref, sem.at[1], gather=False)
    _wait_rows(h_ref, xs_ref, sem.at[0], gather=False)
    _wait_rows(h_ref, xs_ref, sem.at[1], gather=False)


def _dispatch(h, d1, d2, n_slots):
    n = h.shape[0]
    nt = n // T_ROW
    idx_spec = pl.BlockSpec((1, 1, T_ROW), lambda i: (i, 0, 0), memory_space=pltpu.SMEM)
    return pl.pallas_call(
        _dispatch_body,
        grid=(nt,),
        in_specs=[idx_spec, idx_spec,
                  pl.BlockSpec((T_ROW, D), lambda i: (i, 0)),
                  pl.BlockSpec(memory_space=pl.ANY)],
        out_specs=pl.BlockSpec(memory_space=pl.ANY),
        out_shape=jax.ShapeDtypeStruct((n_slots, D), F32),
        scratch_shapes=[pltpu.SemaphoreType.DMA((2,))],
        input_output_aliases={3: 0},
        compiler_params=pltpu.CompilerParams(dimension_semantics=("arbitrary",)),
        name="moe_dispatch",
    )(d1.reshape(nt, 1, T_ROW), d2.reshape(nt, 1, T_ROW), h, jnp.zeros((n_slots, D), F32))


def _ffn_moe_body(te_ref, nv_ref, xs_ref, w1_ref, w3_ref, w2_ref, o_ref, xb_ref, acc_ref):
    del te_ref
    i = pl.program_id(0)
    j = pl.program_id(1)
    valid = i < nv_ref[0]

    @pl.when(jnp.logical_and(valid, j == 0))
    def _():
        xb_ref[...] = xs_ref[...].astype(BF16)
        acc_ref[...] = jnp.zeros_like(acc_ref)

    @pl.when(valid)
    def _():
        h = xb_ref[...]
        a = jnp.dot(h, w1_ref[0], preferred_element_type=F32)
        b = jnp.dot(h, w3_ref[0], preferred_element_type=F32)
        t = (_silu(a) * b).astype(BF16)
        acc_ref[...] += jnp.dot(t, w2_ref[0], preferred_element_type=F32)

    last = j == pl.num_programs(1) - 1

    @pl.when(jnp.logical_and(valid, last))
    def _():
        o_ref[...] = acc_ref[...]

    @pl.when(jnp.logical_and(jnp.logical_not(valid), last))
    def _():
        o_ref[...] = jnp.zeros_like(o_ref)


def _ffn_moe(xs, tile_expert, n_valid, w1, w3, w2):
    n_slots = xs.shape[0]
    nt = n_slots // T_MOE
    nj = D_FF // T_FF

    def jj(i, j, nv):
        return jnp.where(i < nv[0], j, nj - 1)

    grid_spec = pltpu.PrefetchScalarGridSpec(
        num_scalar_prefetch=2,
        grid=(nt, nj),
        in_specs=[
            pl.BlockSpec((T_MOE, D), lambda i, j, te, nv: (i, 0)),
            pl.BlockSpec((1, D, T_FF), lambda i, j, te, nv: (te[i], 0, jj(i, j, nv))),
            pl.BlockSpec((1, D, T_FF), lambda i, j, te, nv: (te[i], 0, jj(i, j, nv))),
            pl.BlockSpec((1, T_FF, D), lambda i, j, te, nv: (te[i], jj(i, j, nv), 0)),
        ],
        out_specs=pl.BlockSpec((T_MOE, D), lambda i, j, te, nv: (i, 0)),
        scratch_shapes=[pltpu.VMEM((T_MOE, D), BF16), pltpu.VMEM((T_MOE, D), F32)],
    )
    return pl.pallas_call(
        _ffn_moe_body,
        grid_spec=grid_spec,
        out_shape=jax.ShapeDtypeStruct((n_slots, D), F32),
        compiler_params=pltpu.CompilerParams(
            dimension_semantics=("arbitrary", "arbitrary"), vmem_limit_bytes=VMEM_BIG),
        name="ffn_moe",
    )(tile_expert, n_valid, xs, w1, w3, w2)


def _combine_body(d1_ref, d2_ref, x_ref, gate_ref, w1_ref, w2_ref, os_ref, o_ref, r1_ref, r2_ref, sem):
    _row_copies(os_ref, r1_ref, d1_ref, sem.at[0], gather=True)
    _row_copies(os_ref, r2_ref, d2_ref, sem.at[1], gather=True)
    _wait_rows(os_ref, r1_ref, sem.at[0], gather=True)
    _wait_rows(os_ref, r2_ref, sem.at[1], gather=True)
    y = w1_ref[...] * r1_ref[...] + w2_ref[...] * r2_ref[...]
    o_ref[...] = x_ref[...] + gate_ref[0] * y


def _combine(x, n, gate, d1, d2, w1, w2, o_sorted):
    nt = n // T_ROW
    idx_spec = pl.BlockSpec((1, 1, T_ROW), lambda i: (i, 0, 0), memory_space=pltpu.SMEM)
    col_spec = pl.BlockSpec((T_ROW, 1), lambda i: (i, 0))
    return pl.pallas_call(
        _combine_body,
        grid=(nt,),
        in_specs=[idx_spec, idx_spec,
                  pl.BlockSpec((T_ROW, D), lambda i: (i, 0)),
                  _mod_spec(T_ROW),
                  col_spec, col_spec,
                  pl.BlockSpec(memory_space=pl.ANY)],
        out_specs=pl.BlockSpec((T_ROW, D), lambda i: (i, 0)),
        out_shape=jax.ShapeDtypeStruct((n, D), F32),
        scratch_shapes=[pltpu.VMEM((T_ROW, D), F32), pltpu.VMEM((T_ROW, D), F32),
                        pltpu.SemaphoreType.DMA((2,))],
        compiler_params=pltpu.CompilerParams(dimension_semantics=("arbitrary",)),
        name="moe_combine",
    )(d1.reshape(nt, 1, T_ROW), d2.reshape(nt, 1, T_ROW), x, gate,
      w1.reshape(n, 1), w2.reshape(n, 1), o_sorted)


def _moe(x, n, sh, sc, gate, ng, router, w1, w3, w2):
    h, i1, i2, p1, p2, r1, r2, cnt = _router(x, n, sh, sc, ng, router)
    i1, i2, r1, r2 = (a.reshape(n) for a in (i1, i2, r1, r2))
    counts = cnt[:, 0].astype(I32)
    padded = ((counts + T_MOE - 1) // T_MOE) * T_MOE
    ends = jnp.cumsum(padded)
    starts = ends - padded
    d1 = starts[i1] + r1
    d2 = starts[i2] + r2
    n_tiles = (2 * n) // T_MOE + N_EXPERTS
    tile_ends = ends // T_MOE
    n_valid = tile_ends[-1]
    tile_ids = jnp.minimum(jnp.arange(n_tiles, dtype=I32), n_valid - 1)
    tile_expert = jnp.sum((tile_ids[:, None] >= tile_ends[None, :]).astype(I32), axis=1)
    xs = _dispatch(h, d1, d2, n_tiles * T_MOE)
    o_sorted = _ffn_moe(xs, tile_expert.astype(I32), n_valid.reshape(1).astype(I32), w1, w3, w2)
    return _combine(x, n, gate, d1, d2, p1.reshape(n), p2.reshape(n), o_sorted)


def _qkv_body(x_ref, sh_ref, sc_ref, ng_ref, w_ref, qg_ref, kg_ref, e_ref, cos_ref, sa_ref, sb_ref,
              q_ref, k_ref, v_ref):
    h = _modnorm(x_ref[...], ng_ref[...], sc_ref[0], sh_ref[0]).astype(BF16)
    z = jnp.dot(h, w_ref[...], preferred_element_type=F32)
    cos = cos_ref[...]
    sa = sa_ref[...]
    sb = sb_ref[...]
    nq = D // LANES
    nk = N_KV
    for jb in range(nq + nk):
        zz = z[:, jb * LANES:(jb + 1) * LANES]
        ss = jnp.dot((zz * zz).astype(BF16), e_ref[...], preferred_element_type=F32) * (1.0 / HEAD_DIM)
        g = qg_ref[...] if jb < nq else kg_ref[...]
        yn = zz * lax.rsqrt(ss + EPS) * g
        rot = yn * cos + pltpu.roll(yn, LANES - 16, 1) * sa + pltpu.roll(yn, 16, 1) * sb
        if jb < nq:
            q_ref[:, jb * LANES:(jb + 1) * LANES] = (rot * (HEAD_DIM ** -0.5)).astype(BF16)
        else:
            k_ref[:, (jb - nq) * LANES:(jb - nq + 1) * LANES] = rot.astype(BF16)
    v_ref[...] = z[:, (nq + nk) * LANES:].astype(BF16)


def _rope_tables():
    half = HEAD_DIM // 2
    pos = jnp.arange(SEQ)
    pos_row = (pos // GRID_W).astype(F32)
    pos_col = (pos % GRID_W).astype(F32)
    inv_freq = ROPE_THETA ** (-jnp.arange(0, half, 2, dtype=F32) / half)
    ang_row = pos_row[:, None] * inv_freq[None, :]
    ang_col = pos_col[:, None] * inv_freq[None, :]
    ang = jnp.concatenate([ang_row, ang_row, ang_col, ang_col], axis=1)
    ang = jnp.tile(ang, (1, LANES // HEAD_DIM))
    first = (jnp.arange(LANES) % half) < half // 2
    cos = jnp.cos(ang).astype(F32)
    sin = jnp.sin(ang).astype(F32)
    sa = jnp.where(first[None, :], -sin, 0.0)
    sb = jnp.where(first[None, :], 0.0, sin)
    ident = jnp.zeros((T_QKV, LANES), F32)
    return (jnp.concatenate([cos, ident + 1.0], axis=0),
            jnp.concatenate([sa, ident], axis=0),
            jnp.concatenate([sb, ident], axis=0))


def _qkv(x, sh, sc, ng, w_qkv, q_g, k_g, tables):
    n = x.shape[0]
    nt = n // T_QKV
    nq = N_HEADS * HEAD_DIM
    nkv = N_KV * HEAD_DIM
    dup = lambda w: jnp.concatenate([w.reshape(D, N_KV, HEAD_DIM)] * 2, axis=2).reshape(D, 2 * nkv)
    w = jnp.concatenate([w_qkv[:, :nq], dup(w_qkv[:, nq:nq + nkv]), dup(w_qkv[:, nq + nkv:])],
                        axis=1).astype(BF16)
    wd = w.shape[1]
    lane_head = jnp.arange(LANES) // HEAD_DIM
    ones_blk = (lane_head[:, None] == lane_head[None, :]).astype(BF16)
    per_seq = SEQ // T_QKV
    tab_spec = pl.BlockSpec(
        (T_QKV, LANES), lambda i: (jnp.where(i < N_LAT // T_QKV, i % per_seq, per_seq), 0))
    mod = _mod_spec(T_QKV)
    tile_g = lambda g: jnp.tile(g, LANES // HEAD_DIM).reshape(1, LANES)
    return pl.pallas_call(
        _qkv_body,
        grid=(nt,),
        in_specs=[
            pl.BlockSpec((T_QKV, D), lambda i: (i, 0)),
            mod, mod,
            _const_spec((1, D)),
            _const_spec((D, wd)),
            _const_spec((1, LANES)), _const_spec((1, LANES)),
            _const_spec((LANES, LANES)),
            tab_spec, tab_spec, tab_spec,
        ],
        out_specs=[pl.BlockSpec((T_QKV, D), lambda i: (i, 0)),
                   pl.BlockSpec((T_QKV, 2 * nkv), lambda i: (i, 0)),
                   pl.BlockSpec((T_QKV, 2 * nkv), lambda i: (i, 0))],
        out_shape=[jax.ShapeDtypeStruct((n, D), BF16),
                   jax.ShapeDtypeStruct((n, 2 * nkv), BF16),
                   jax.ShapeDtypeStruct((n, 2 * nkv), BF16)],
        compiler_params=pltpu.CompilerParams(
            dimension_semantics=("arbitrary",), vmem_limit_bytes=VMEM_BIG),
        name="attn_qkv",
    )(x, sh, sc, ng.reshape(1, D), w, tile_g(q_g), tile_g(k_g), ones_blk, *tables)


def _attn_heads(q, k_all, v_all, mask, sink_ref, o_ref):
    lane = lax.broadcasted_iota(I32, (1, LANES), 1)
    low = lane < HEAD_DIM
    group = N_HEADS // N_KV
    zero = jnp.zeros((), BF16)
    for kh in range(N_KV):
        kk = k_all[:, kh * LANES:(kh + 1) * LANES]
        vv = v_all[:, kh * LANES:(kh + 1) * LANES]
        parts = []
        sinks = []
        for g in range(group):
            hd = kh * group + g
            qb = q[:, (hd // 2) * LANES:(hd // 2 + 1) * LANES]
            keep = low if hd % 2 == 0 else jnp.logical_not(low)
            parts.append(jnp.where(keep, qb, zero))
            sinks.append(jnp.full((T_Q, 1), sink_ref[hd], F32))
        q4 = jnp.concatenate(parts, axis=0)
        sink = jnp.concatenate(sinks, axis=0)
        s = lax.dot_general(q4, kk, (((1,), (1,)), ((), ())), preferred_element_type=F32)
        if mask is not None:
            s = jnp.where(mask, s, NEG_INF)
        m = jnp.maximum(jnp.max(s, axis=-1, keepdims=True), sink)
        p = jnp.exp(s - m)
        den = jnp.sum(p, axis=-1, keepdims=True) + jnp.exp(sink - m)
        o4 = jnp.dot(p.astype(BF16), vv, preferred_element_type=F32) * (1.0 / den)
        for pair in range(group // 2):
            even = o4[(2 * pair) * T_Q:(2 * pair + 1) * T_Q]
            odd = o4[(2 * pair + 1) * T_Q:(2 * pair + 2) * T_Q]
            jb = kh * (group // 2) + pair
            o_ref[:, jb * LANES:(jb + 1) * LANES] = jnp.where(low, even, odd).astype(BF16)


def _attn_body(sink_ref, q_ref, kp_ref, kc_ref, kn_ref, kx_ref, vp_ref, vc_ref, vn_ref, vx_ref, o_ref,
               *, n_q_blocks):
    iq = pl.program_id(1)
    per_seq = SEQ // T_Q
    group = N_HEADS // N_KV

    @pl.when(iq < per_seq)
    def _():
        k_all = jnp.concatenate([kp_ref[...], kc_ref[...], kn_ref[...], kx_ref[...]], axis=0)
        v_all = jnp.concatenate([vp_ref[...], vc_ref[...], vn_ref[...], vx_ref[...]], axis=0)
        nk = 3 * T_Q + CTX_LEN
        r = lax.broadcasted_iota(I32, (group * T_Q, nk), 0) % T_Q
        c = lax.broadcasted_iota(I32, (group * T_Q, nk), 1)
        rel = c - T_Q - r
        in_band = jnp.logical_and(rel >= -WINDOW, rel <= WINDOW)
        c_lo = jnp.where(iq > 0, 0, T_Q)
        c_hi = jnp.where(iq < per_seq - 1, 3 * T_Q, 2 * T_Q)
        kpos_ok = jnp.logical_and(c >= c_lo, c < c_hi)
        mask = jnp.logical_or(c >= 3 * T_Q, jnp.logical_and(in_band, kpos_ok))
        _attn_heads(q_ref[...], k_all, v_all, mask, sink_ref, o_ref)

    if n_q_blocks > per_seq:
        @pl.when(iq >= per_seq)
        def _():
            _attn_heads(q_ref[...], kx_ref[...], vx_ref[...], None, sink_ref, o_ref)


def _attention(q, k2, v2, sink, need_ctx):
    per_seq = SEQ // T_Q
    ctx_blocks = CTX_LEN // T_Q
    n_q_blocks = per_seq + (ctx_blocks if need_ctx else 0)
    n_out = N_ALL if need_ctx else N_LAT
    lat_blocks = N_LAT // T_Q
    kw = k2.shape[1]

    def q_idx(b, iq, *_):
        return (jnp.where(iq < per_seq, b * per_seq + iq,
                          lat_blocks + b * ctx_blocks + (iq - per_seq)), 0)

    def win_idx(off):
        return lambda b, iq, *_: (b * per_seq + jnp.clip(iq + off, 0, per_seq - 1), 0)

    ctx_idx = lambda b, iq, *_: (N_LAT // CTX_LEN + b, 0)
    win = lambda off: pl.BlockSpec((T_Q, kw), win_idx(off))
    ctx = pl.BlockSpec((CTX_LEN, kw), ctx_idx)
    grid_spec = pltpu.PrefetchScalarGridSpec(
        num_scalar_prefetch=1,
        grid=(NB, n_q_blocks),
        in_specs=[pl.BlockSpec((T_Q, D), q_idx),
                  win(-1), win(0), win(1), ctx, win(-1), win(0), win(1), ctx],
        out_specs=pl.BlockSpec((T_Q, D), q_idx),
    )
    return pl.pallas_call(
        functools.partial(_attn_body, n_q_blocks=n_q_blocks),
        grid_spec=grid_spec,
        out_shape=jax.ShapeDtypeStruct((n_out, D), BF16),
        compiler_params=pltpu.CompilerParams(
            dimension_semantics=("arbitrary", "arbitrary"), vmem_limit_bytes=VMEM_BIG),
        name="attn_core",
    )(sink.astype(F32), q, k2, k2, k2, k2, v2, v2, v2, v2)


def _proj_body(a_ref, x_ref, gate_ref, w_ref, o_ref):
    y = jnp.dot(a_ref[...], w_ref[...], preferred_element_type=F32)
    o_ref[...] = x_ref[...] + gate_ref[0] * y


def _proj_residual(a, x, gate, w):
    n = a.shape[0]
    return pl.pallas_call(
        _proj_body,
        grid=(n // T_PROJ,),
        in_specs=[pl.BlockSpec((T_PROJ, D), lambda i: (i, 0)),
                  pl.BlockSpec((T_PROJ, D), lambda i: (i, 0)),
                  _mod_spec(T_PROJ),
                  _const_spec((D, D))],
        out_specs=pl.BlockSpec((T_PROJ, D), lambda i: (i, 0)),
        out_shape=jax.ShapeDtypeStruct((n, D), F32),
        compiler_params=pltpu.CompilerParams(dimension_semantics=("arbitrary",)),
        name="attn_proj",
    )(a, x, gate, w.astype(BF16))


def kernel(x, c, ctx, c_ctx, ada_w, ada_b, norm_mix_g, norm_ffn_g, ev_w_in, ev_ln_g, ev_ln_b, ev_ws,
           ev_bs, ev_conv_w, ev_conv_b, ev_cnorm_g, ev_w_out, od_w_qkv, od_q_g, od_k_g, od_sink, od_w_o,
           ff_w1, ff_w3, ff_w2, moe_router, moe_w1, moe_w3, moe_w2):
    assert x.shape == (NB, SEQ, D) and ctx.shape == (NB, CTX_LEN, D)
    mods = _ada_mods(c, c_ctx, ada_w, ada_b)
    tables = _rope_tables()
    xa = jnp.concatenate([x.reshape(N_LAT, D), ctx.reshape(N_CTX, D)], axis=0)
    for li in range(DEPTH):
        need_ctx = li < DEPTH - 1
        j = li // 2
        sh1, sc1, g1, sh2, sc2, g2 = mods[li]
        if li % 2 == 0:
            xa = _even_mixer(xa, sh1, sc1, g1, norm_mix_g[li], ev_w_in[j], ev_ln_g[j], ev_ln_b[j],
                             ev_ws[j], ev_bs[j], ev_conv_w[j], ev_conv_b[j], ev_cnorm_g[j], ev_w_out[j])
            xa = _ffn_dense(xa, sh2, sc2, g2, norm_ffn_g[li], ff_w1[j], ff_w3[j], ff_w2[j])
        else:
            q, k2, v2 = _qkv(xa, sh1, sc1, norm_mix_g[li], od_w_qkv[j], od_q_g[j], od_k_g[j], tables)
            o = _attention(q, k2, v2, od_sink[j], need_ctx)
            xa = _proj_residual(o, xa, g1, od_w_o[j])
            n = N_ALL if need_ctx else N_LAT
            xa = _moe(xa, n, sh2, sc2, g2, norm_ffn_g[li], moe_router[j],
                      moe_w1[j].astype(BF16), moe_w3[j].astype(BF16), moe_w2[j].astype(BF16))
    return xa[:N_LAT].reshape(NB, SEQ, D)
```

```python
import functools

import jax
import jax.numpy as jnp
from jax import lax
from jax.experimental import pallas as pl
from jax.experimental.pallas import tpu as pltpu

F32 = jnp.float32
BF16 = jnp.bfloat16
I32 = jnp.int32

D = 1024
NB = 8
SEQ = 2048
CTX_LEN = 256
DEPTH = 4
GRID_W = 64
CHUNK = 128
A_GROUPS = 8
CONV_W = 31
N_HEADS = 16
N_KV = 4
HEAD_DIM = 64
WINDOW = 128
ROPE_THETA = 10000.0
D_FF = 3584
N_EXPERTS = 8
EPS = 1e-6
NEG_INF = -1e30
LOG2_E = 1.4426950408889634

N_LAT = NB * SEQ
N_CTX = NB * CTX_LEN
N_ALL = N_LAT + N_CTX
CTX_MOD_ROW = NB

LANES = 128
SUBLANES = 8
BF16_ROWS = 16

T_EVEN = 256
HALO = BF16_ROWS
T_FFN = 1024
T_FF = 512
T_ROUTE = 512
T_MOE = 512
SEG_ALIGN = SUBLANES
S_LOC = 2 * T_ROUTE + N_EXPERTS * SEG_ALIGN
SEG_SIZES = tuple(T_ROUTE >> s for s in range(7))
assert SEG_SIZES[-1] == SEG_ALIGN
T_QKV = 256
T_Q = 128
T_PROJ = 512

VMEM_BIG = 52 * 1024 * 1024


def _mod_spec(tile):
    per = SEQ // tile
    return pl.BlockSpec((1, 1, D), lambda i, *_: (jnp.minimum(i // per, CTX_MOD_ROW), 0, 0))


def _const_spec(shape):
    nd = len(shape)
    return pl.BlockSpec(shape, lambda *_: (0,) * nd)


def _modnorm(x, g, sc, sh):
    ms = jnp.mean(x * x, axis=-1, keepdims=True)
    return (x * lax.rsqrt(ms + EPS) * g) * (1.0 + sc) + sh


def _silu(x):
    return x * jax.nn.sigmoid(x)


def _ada_body(c_ref, w_ref, b_ref, o_ref):
    a = _silu(c_ref[...])
    o_ref[0] = jnp.dot(a.astype(BF16), w_ref[0].astype(BF16), preferred_element_type=F32) + b_ref[0]


def _ada_mods(c, c_ctx, ada_w, ada_b):
    rows = 16
    cc = jnp.concatenate([c, c_ctx[None, :], jnp.zeros((rows - NB - 1, D), F32)], axis=0)
    out = pl.pallas_call(
        _ada_body,
        grid=(DEPTH, 6),
        in_specs=[
            pl.BlockSpec((rows, D), lambda l, n: (0, 0)),
            pl.BlockSpec((1, D, D), lambda l, n: (l, 0, n)),
            pl.BlockSpec((1, 1, D), lambda l, n: (l, 0, n)),
        ],
        out_specs=pl.BlockSpec((1, rows, D), lambda l, n: (l, 0, n)),
        out_shape=jax.ShapeDtypeStruct((DEPTH, rows, 6 * D), F32),
        name="ada_mod",
    )(cc, ada_w, ada_b.reshape(DEPTH, 1, 6 * D))
    m = out[:, :NB + 1].reshape(DEPTH, NB + 1, 6, 1, D)
    return [[m[l, :, k] for k in range(6)] for l in range(DEPTH)]


def _even_body(x_ref, xp_ref, xn_ref, sh_ref, sc_ref, gate_ref, ng_ref, win_ref, lng_ref, lnb_ref,
               ws_ref, bs_ref, cw_ref, cb_ref, cng_ref, wout_ref, o_ref, gext_ref, cv_ref):
    i = pl.program_id(0)
    per_seq = SEQ // T_EVEN
    is_lat = i < N_LAT // T_EVEN
    pos = i % per_seq
    is_start = jnp.logical_or(jnp.logical_not(is_lat), pos == 0)
    is_end = jnp.logical_or(jnp.logical_not(is_lat), pos == per_seq - 1)
    ext = T_EVEN + 2 * HALO

    xm = x_ref[...]
    xt = jnp.concatenate([xp_ref[...], xm, xn_ref[...]], axis=0)
    h = _modnorm(xt, ng_ref[...], sc_ref[0], sh_ref[0]).astype(BF16)

    zb = jnp.dot(h, win_ref[:, 2 * D:], preferred_element_type=F32)
    za = jnp.dot(h[HALO:HALO + T_EVEN], win_ref[:, :2 * D], preferred_element_type=F32)
    gg = zb[:, :D] * jax.nn.sigmoid(zb[:, D:])
    row = lax.broadcasted_iota(I32, (ext, 1), 0)
    lo = jnp.where(is_start, HALO, 0)
    hi = jnp.where(is_end, T_EVEN + HALO, ext)
    gg = jnp.where(jnp.logical_and(row >= lo, row < hi), gg, 0.0)
    for cbk in range(D // LANES):
        gext_ref[cbk] = gg[:, cbk * LANES:(cbk + 1) * LANES]

    rows_per = 64

    def conv_block(cbk, carry):
        for rb in range(T_EVEN // rows_per):
            acc = jnp.zeros((rows_per, LANES), F32)
            for k in range(CONV_W):
                off = rb * rows_per + k + HALO - CONV_W // 2
                acc = acc + cw_ref[cbk, pl.ds(k, 1), :] * gext_ref[cbk, pl.ds(off, rows_per), :]
            cv_ref[cbk, pl.ds(rb * rows_per, rows_per), :] = acc
        return carry

    for cbk in range(D // LANES):
        conv_block(cbk, 0)
    cv = jnp.concatenate([cv_ref[cbk] for cbk in range(D // LANES)], axis=1) + cb_ref[...]
    ms = jnp.mean(cv * cv, axis=-1, keepdims=True)
    yb = _silu(cv * lax.rsqrt(ms + EPS) * cng_ref[...])

    u = jax.nn.gelu(za[:, :D])
    v = jax.nn.gelu(za[:, D:])
    mu = jnp.mean(v, axis=-1, keepdims=True)
    vc = v - mu
    var = jnp.mean(vc * vc, axis=-1, keepdims=True)
    vn = (vc * lax.rsqrt(var + EPS) * lng_ref[...] + lnb_ref[...]).astype(BF16)
    gdim = D // A_GROUPS
    chunks = []
    for ck in range(T_EVEN // CHUNK):
        blocks = [
            jnp.dot(ws_ref[g], vn[ck * CHUNK:(ck + 1) * CHUNK, g * gdim:(g + 1) * gdim],
                    preferred_element_type=F32)
            for g in range(A_GROUPS)
        ]
        chunks.append(jnp.concatenate(blocks, axis=1) + bs_ref[...])
    ya = u * jnp.concatenate(chunks, axis=0)

    y = (jnp.dot(ya.astype(BF16), wout_ref[:D], preferred_element_type=F32)
         + jnp.dot(yb.astype(BF16), wout_ref[D:], preferred_element_type=F32))
    o_ref[...] = xm + gate_ref[0] * y


def _even_mixer(x, sh, sc, gate, ng, w_in, ln_g, ln_b, ws, bs, conv_w, conv_b, cn_g, w_out):
    n = x.shape[0]
    nt = n // T_EVEN
    hb = T_EVEN // HALO
    last = n // HALO - 1
    ncb = D // LANES
    bs_full = jnp.repeat(bs.T, D // A_GROUPS, axis=1)
    cw = jnp.pad(conv_w, ((0, 32 - CONV_W), (0, 0))).reshape(32, ncb, LANES).transpose(1, 0, 2)
    mod = _mod_spec(T_EVEN)
    row1 = lambda a: a.reshape(1, D)
    return pl.pallas_call(
        _even_body,
        grid=(nt,),
        in_specs=[
            pl.BlockSpec((T_EVEN, D), lambda i: (i, 0)),
            pl.BlockSpec((HALO, D), lambda i: (jnp.maximum(i * hb - 1, 0), 0)),
            pl.BlockSpec((HALO, D), lambda i: (jnp.minimum((i + 1) * hb, last), 0)),
            mod, mod, mod,
            _const_spec((1, D)),
            _const_spec((D, 4 * D)),
            _const_spec((1, D)), _const_spec((1, D)),
            _const_spec((A_GROUPS, CHUNK, CHUNK)),
            _const_spec((CHUNK, D)),
            _const_spec((ncb, 32, LANES)),
            _const_spec((1, D)), _const_spec((1, D)),
            _const_spec((2 * D, D)),
        ],
        out_specs=pl.BlockSpec((T_EVEN, D), lambda i: (i, 0)),
        out_shape=jax.ShapeDtypeStruct((n, D), F32),
        scratch_shapes=[
            pltpu.VMEM((ncb, T_EVEN + 2 * HALO, LANES), F32),
            pltpu.VMEM((ncb, T_EVEN, LANES), F32),
        ],
        compiler_params=pltpu.CompilerParams(
            dimension_semantics=("arbitrary",), vmem_limit_bytes=VMEM_BIG),
        name="even_mixer",
    )(x, x, x, sh, sc, gate, row1(ng), w_in.astype(BF16), row1(ln_g), row1(ln_b), ws.astype(BF16),
      bs_full, cw, row1(conv_b), row1(cn_g), w_out.astype(BF16))


def _ffn_dense_body(x_ref, sh_ref, sc_ref, gate_ref, ng_ref, w1_ref, w3_ref, w2_ref, o_ref,
                    h_ref, acc_ref):
    j = pl.program_id(1)

    @pl.when(j == 0)
    def _():
        h_ref[...] = _modnorm(x_ref[...], ng_ref[...], sc_ref[0], sh_ref[0]).astype(BF16)
        acc_ref[...] = jnp.zeros_like(acc_ref)

    h = h_ref[...]
    a = jnp.dot(h, w1_ref[...].astype(BF16), preferred_element_type=F32)
    b = jnp.dot(h, w3_ref[...].astype(BF16), preferred_element_type=F32)
    t = (_silu(a) * b).astype(BF16)
    acc_ref[...] += jnp.dot(t, w2_ref[...].astype(BF16), preferred_element_type=F32)

    @pl.when(j == pl.num_programs(1) - 1)
    def _():
        o_ref[...] = x_ref[...] + gate_ref[0] * acc_ref[...]


def _ffn_dense(x, sh, sc, gate, ng, w1, w3, w2):
    n = x.shape[0]
    mod = _mod_spec(T_FFN)
    return pl.pallas_call(
        _ffn_dense_body,
        grid=(n // T_FFN, D_FF // T_FF),
        in_specs=[
            pl.BlockSpec((T_FFN, D), lambda i, j: (i, 0)),
            mod, mod, mod,
            _const_spec((1, D)),
            pl.BlockSpec((D, T_FF), lambda i, j: (0, j)),
            pl.BlockSpec((D, T_FF), lambda i, j: (0, j)),
            pl.BlockSpec((T_FF, D), lambda i, j: (j, 0)),
        ],
        out_specs=pl.BlockSpec((T_FFN, D), lambda i, j: (i, 0)),
        out_shape=jax.ShapeDtypeStruct((n, D), F32),
        scratch_shapes=[pltpu.VMEM((T_FFN, D), BF16), pltpu.VMEM((T_FFN, D), F32)],
        compiler_params=pltpu.CompilerParams(
            dimension_semantics=("arbitrary", "arbitrary"), vmem_limit_bytes=VMEM_BIG),
        name="ffn_dense",
    )(x, sh, sc, gate, ng.reshape(1, D), w1, w3, w2)


def _router_body(x_ref, sh_ref, sc_ref, ng_ref, rt_ref, s1_ref, s2_ref, w1_ref, w2_ref, pc_ref):
    h = _modnorm(x_ref[...], ng_ref[...], sc_ref[0], sh_ref[0])
    lg = lax.dot_general(rt_ref[...], h, (((1,), (1,)), ((), ())),
                         precision=lax.Precision.HIGHEST, preferred_element_type=F32)
    e = lax.broadcasted_iota(I32, lg.shape, 0).astype(F32)
    big = float(N_EXPERTS)
    m1 = jnp.max(lg, axis=0, keepdims=True)
    i1 = jnp.min(jnp.where(lg == m1, e, big), axis=0, keepdims=True)
    lg2 = jnp.where(e == i1, -jnp.inf, lg)
    m2 = jnp.max(lg2, axis=0, keepdims=True)
    i2 = jnp.min(jnp.where(lg2 == m2, e, big), axis=0, keepdims=True)
    e2 = jnp.exp(m2 - m1)
    den = 1.0 + e2
    w1_ref[0] = 1.0 / den
    w2_ref[0] = e2 / den

    sel1 = e == i1
    sel2 = e == i2
    member = jnp.where(jnp.logical_or(sel1, sel2), 1.0, 0.0)
    t = lg.shape[1]
    before = (lax.broadcasted_iota(I32, (t, t), 0) < lax.broadcasted_iota(I32, (t, t), 1))
    tri = jnp.where(before, 1.0, 0.0).astype(BF16)
    rank = jnp.dot(member.astype(BF16), tri, preferred_element_type=F32)
    count = jnp.sum(member, axis=1, keepdims=True)
    padded = jnp.ceil(count * (1.0 / SEG_ALIGN)) * SEG_ALIGN
    e_col = lax.broadcasted_iota(I32, (N_EXPERTS, 1), 0)
    seg_start = jnp.zeros((N_EXPERTS, 1), F32)
    for ex in range(1, N_EXPERTS):
        below = jnp.sum(jnp.where(e_col < ex, padded, 0.0), axis=0, keepdims=True)
        seg_start = jnp.where(e_col == ex, below, seg_start)
    slot = rank + seg_start
    s1 = jnp.sum(jnp.where(sel1, slot, 0.0), axis=0, keepdims=True).astype(I32)
    s2 = jnp.sum(jnp.where(sel2, slot, 0.0), axis=0, keepdims=True).astype(I32)
    s1_ref[0] = s1
    s2_ref[0] = s2
    pc_ref[0] = jnp.broadcast_to(padded.astype(I32), (N_EXPERTS, LANES))


def _router(x, n, sh, sc, ng, router):
    nt = n // T_ROUTE
    mod = _mod_spec(T_ROUTE)
    vec_spec = pl.BlockSpec((1, 1, T_ROUTE), lambda i: (i, 0, 0))
    vec_i = jax.ShapeDtypeStruct((nt, 1, T_ROUTE), I32)
    vec_f = jax.ShapeDtypeStruct((nt, 1, T_ROUTE), F32)
    return pl.pallas_call(
        _router_body,
        grid=(nt,),
        in_specs=[
            pl.BlockSpec((T_ROUTE, D), lambda i: (i, 0)),
            mod, mod,
            _const_spec((1, D)),
            _const_spec((N_EXPERTS, D)),
        ],
        out_specs=[vec_spec, vec_spec, vec_spec, vec_spec,
                   pl.BlockSpec((1, N_EXPERTS, LANES), lambda i: (i, 0, 0))],
        out_shape=[vec_i, vec_i, vec_f, vec_f, jax.ShapeDtypeStruct((nt, N_EXPERTS, LANES), I32)],
        compiler_params=pltpu.CompilerParams(dimension_semantics=("arbitrary",)),
        name="moe_router",
    )(x, sh, sc, ng.reshape(1, D), router.T)


def _dispatch_body(start_ref, len_ref, tail_start_ref, tail_len_ref, nv_ref,
                   x_ref, sh_ref, sc_ref, ng_ref, s1_ref, s2_ref, xs_ref, lbuf_ref, zbuf_ref, sem,
                   *, n_tiles, min_tiles):
    i = pl.program_id(0)
    h = _modnorm(x_ref[...], ng_ref[...], sc_ref[0], sh_ref[0]).astype(BF16)
    rows = lax.broadcasted_iota(I32, (S_LOC, T_ROUTE), 0)
    perm = jnp.where(jnp.logical_or(rows == s1_ref[0], rows == s2_ref[0]), 1.0, 0.0).astype(BF16)
    lbuf_ref[...] = jnp.dot(perm, h, preferred_element_type=F32)

    src = 0
    plan = []
    for ex in range(N_EXPERTS):
        ln = len_ref[i * N_EXPERTS + ex]
        plan.append((ln, src, start_ref[i * N_EXPERTS + ex]))
        src = src + ln
    buf_at = lambda a, p: lbuf_ref.at[pl.ds(a, p)]
    xs_at = lambda a, p: xs_ref.at[pl.ds(a, p)]
    _segment_copies(plan, buf_at, xs_at, sem, start=True)
    _segment_copies(plan, buf_at, xs_at, sem, start=False)

    @pl.when(i == pl.num_programs(0) - 1)
    def _():
        zbuf_ref[...] = jnp.zeros_like(zbuf_ref)
        tails = [(tail_len_ref[ex], 0, tail_start_ref[ex]) for ex in range(N_EXPERTS)]
        zero_at = lambda a, p: zbuf_ref.at[pl.ds(a, p)]
        _segment_copies(tails, zero_at, xs_at, sem, start=True)
        _segment_copies(tails, zero_at, xs_at, sem, start=False)
        for tile in range(min_tiles, n_tiles):
            @pl.when(tile >= nv_ref[0])
            def _(tile=tile):
                cp = pltpu.make_async_copy(zbuf_ref, xs_ref.at[pl.ds(tile * T_MOE, T_MOE)], sem)
                cp.start()
                cp.wait()


def _dispatch(x, n, sh, sc, ng, s1, s2, seg_start, seg_len, tail_start, tail_len, n_valid, n_tiles):
    nt = n // T_ROUTE
    mod = _mod_spec(T_ROUTE)
    vec_spec = pl.BlockSpec((1, 1, T_ROUTE), lambda i, *_: (i, 0, 0))
    grid_spec = pltpu.PrefetchScalarGridSpec(
        num_scalar_prefetch=5,
        grid=(nt,),
        in_specs=[pl.BlockSpec((T_ROUTE, D), lambda i, *_: (i, 0)),
                  mod, mod,
                  pl.BlockSpec((1, D), lambda i, *_: (0, 0)),
                  vec_spec, vec_spec],
        out_specs=pl.BlockSpec(memory_space=pl.ANY),
        scratch_shapes=[pltpu.VMEM((S_LOC, D), F32), pltpu.VMEM((T_MOE, D), F32),
                        pltpu.SemaphoreType.DMA(())],
    )
    return pl.pallas_call(
        functools.partial(_dispatch_body, n_tiles=n_tiles, min_tiles=(2 * n) // T_MOE),
        grid_spec=grid_spec,
        out_shape=jax.ShapeDtypeStruct((n_tiles * T_MOE, D), F32),
        compiler_params=pltpu.CompilerParams(
            dimension_semantics=("arbitrary",), vmem_limit_bytes=VMEM_BIG),
        name="moe_dispatch",
    )(seg_start.reshape(-1), seg_len.reshape(-1), tail_start, tail_len, n_valid,
      x, sh, sc, ng.reshape(1, D), s1, s2)


def _segment_copies(plan, src_at, dst_at, sem, start):
    for ln, src, dst in plan:
        for p in SEG_SIZES:
            off = jnp.bitwise_and(ln, -(2 * p))

            @pl.when(jnp.bitwise_and(ln, p) != 0)
            def _(off=off, src=src, dst=dst, p=p):
                cp = pltpu.make_async_copy(src_at(pl.multiple_of(src + off, SEG_ALIGN), p),
                                           dst_at(pl.multiple_of(dst + off, SEG_ALIGN), p), sem)
                if start:
                    cp.start()
                else:
                    cp.wait()


def _ffn_moe_body(te_ref, nv_ref, xs_ref, w1_ref, w3_ref, w2_ref, o_ref, xb_ref, acc_ref):
    del te_ref
    i = pl.program_id(0)
    j = pl.program_id(1)
    valid = i < nv_ref[0]

    @pl.when(jnp.logical_and(valid, j == 0))
    def _():
        xb_ref[...] = xs_ref[...].astype(BF16)
        acc_ref[...] = jnp.zeros_like(acc_ref)

    @pl.when(valid)
    def _():
        h = xb_ref[...]
        a = jnp.dot(h, w1_ref[0, 0], preferred_element_type=F32)
        b = jnp.dot(h, w3_ref[0, 0], preferred_element_type=F32)
        t = (_silu(a) * b).astype(BF16)
        acc_ref[...] += jnp.dot(t, w2_ref[0], preferred_element_type=F32)

    last = j == pl.num_programs(1) - 1

    @pl.when(jnp.logical_and(valid, last))
    def _():
        o_ref[...] = acc_ref[...]

    @pl.when(jnp.logical_and(jnp.logical_not(valid), last))
    def _():
        o_ref[...] = jnp.zeros_like(o_ref)


def _ffn_moe(xs, tile_expert, n_valid, w1, w3, w2):
    nj = D_FF // T_FF

    def jj(i, j, nv):
        return jnp.where(i < nv[0], j, nj - 1)

    grid_spec = pltpu.PrefetchScalarGridSpec(
        num_scalar_prefetch=2,
        grid=(xs.shape[0] // T_MOE, nj),
        in_specs=[
            pl.BlockSpec((T_MOE, D), lambda i, j, te, nv: (i, 0)),
            pl.BlockSpec((1, 1, D, T_FF), lambda i, j, te, nv: (te[i], jj(i, j, nv), 0, 0)),
            pl.BlockSpec((1, 1, D, T_FF), lambda i, j, te, nv: (te[i], jj(i, j, nv), 0, 0)),
            pl.BlockSpec((1, T_FF, D), lambda i, j, te, nv: (te[i], jj(i, j, nv), 0)),
        ],
        out_specs=pl.BlockSpec((T_MOE, D), lambda i, j, te, nv: (i, 0)),
        scratch_shapes=[pltpu.VMEM((T_MOE, D), BF16), pltpu.VMEM((T_MOE, D), F32)],
    )
    return pl.pallas_call(
        _ffn_moe_body,
        grid_spec=grid_spec,
        out_shape=jax.ShapeDtypeStruct(xs.shape, F32),
        compiler_params=pltpu.CompilerParams(
            dimension_semantics=("arbitrary", "arbitrary"), vmem_limit_bytes=VMEM_BIG),
        name="ffn_moe",
    )(tile_expert, n_valid, xs, w1, w3, w2)


def _combine_body(start_ref, len_ref, s1_ref, s2_ref, w1_ref, w2_ref, x_ref, gate_ref, os_ref, o_ref,
                  lo_ref, sem):
    i = pl.program_id(0)

    @pl.when(i == 0)
    def _():
        lo_ref[...] = jnp.zeros_like(lo_ref)

    dst = 0
    plan = []
    for ex in range(N_EXPERTS):
        ln = len_ref[i * N_EXPERTS + ex]
        plan.append((ln, start_ref[i * N_EXPERTS + ex], dst))
        dst = dst + ln
    _segment_copies(plan, lambda a, p: os_ref.at[pl.ds(a, p)], lambda a, p: lo_ref.at[pl.ds(a, p)],
                    sem, start=True)
    _segment_copies(plan, lambda a, p: os_ref.at[pl.ds(a, p)], lambda a, p: lo_ref.at[pl.ds(a, p)],
                    sem, start=False)

    lo = lo_ref[...].astype(BF16)
    lane = lax.broadcasted_iota(I32, (T_ROUTE, S_LOC), 1)
    pick1 = jnp.where(lane == s1_ref[...], 1.0, 0.0).astype(BF16)
    pick2 = jnp.where(lane == s2_ref[...], 1.0, 0.0).astype(BF16)
    y = (w1_ref[...] * jnp.dot(pick1, lo, preferred_element_type=F32)
         + w2_ref[...] * jnp.dot(pick2, lo, preferred_element_type=F32))
    o_ref[...] = x_ref[...] + gate_ref[0] * y


def _combine(x, n, gate, seg_start, seg_len, s1, s2, w1, w2, o_sorted):
    nt = n // T_ROUTE
    col = lambda a: a.reshape(n, 1)
    col_spec = pl.BlockSpec((T_ROUTE, 1), lambda i, *_: (i, 0))
    grid_spec = pltpu.PrefetchScalarGridSpec(
        num_scalar_prefetch=2,
        grid=(nt,),
        in_specs=[col_spec, col_spec, col_spec, col_spec,
                  pl.BlockSpec((T_ROUTE, D), lambda i, *_: (i, 0)),
                  _mod_spec(T_ROUTE),
                  pl.BlockSpec(memory_space=pl.ANY)],
        out_specs=pl.BlockSpec((T_ROUTE, D), lambda i, *_: (i, 0)),
        scratch_shapes=[pltpu.VMEM((S_LOC, D), F32), pltpu.SemaphoreType.DMA(())],
    )
    return pl.pallas_call(
        _combine_body,
        grid_spec=grid_spec,
        out_shape=jax.ShapeDtypeStruct((n, D), F32),
        compiler_params=pltpu.CompilerParams(
            dimension_semantics=("arbitrary",), vmem_limit_bytes=VMEM_BIG),
        name="moe_combine",
    )(seg_start.reshape(-1), seg_len.reshape(-1), col(s1), col(s2), col(w1), col(w2), x, gate, o_sorted)


def _moe(x, n, sh, sc, gate, ng, router, w1, w3, w2):
    nt = n // T_ROUTE
    s1, s2, p1, p2, pc = _router(x, n, sh, sc, ng, router)
    seg_len = pc[:, :, 0]
    total = jnp.sum(seg_len, axis=0)
    tiles_per = (total + T_MOE - 1) // T_MOE
    tile_ends = jnp.cumsum(tiles_per)
    group_start = (tile_ends - tiles_per) * T_MOE
    seg_start = group_start[None, :] + jnp.cumsum(seg_len, axis=0) - seg_len
    tail_start = group_start + total
    tail_len = tiles_per * T_MOE - total
    n_valid = tile_ends[-1:]
    n_tiles = -(-(2 * n + nt * N_EXPERTS * SEG_ALIGN) // T_MOE) + N_EXPERTS
    tile_ids = jnp.minimum(jnp.arange(n_tiles, dtype=I32), n_valid - 1)
    tile_expert = jnp.sum((tile_ids[:, None] >= tile_ends[None, :]).astype(I32), axis=1)
    xs = _dispatch(x, n, sh, sc, ng, s1, s2, seg_start, seg_len, tail_start, tail_len, n_valid, n_tiles)
    o_sorted = _ffn_moe(xs, tile_expert, n_valid, w1, w3, w2)
    return _combine(x, n, gate, seg_start, seg_len, s1, s2, p1, p2, o_sorted)


def _swap_halves(y):
    q = HEAD_DIM // 4
    return jnp.concatenate([y[q:2 * q], y[:q], y[3 * q:], y[2 * q:3 * q]], axis=0)


def _norm_rope_t(zh, c_tab, s_tab):
    ss = jnp.mean(zh * zh, axis=0, keepdims=True)
    yn = zh * lax.rsqrt(ss + EPS)
    return yn * c_tab + _swap_halves(yn) * s_tab


def _qkv_body(x_ref, sh_ref, sc_ref, ng_ref, w_ref, cq_ref, sq_ref, ck_ref, sk_ref, q_ref, k_ref, v_ref):
    h = _modnorm(x_ref[...], ng_ref[...], sc_ref[0], sh_ref[0]).astype(BF16)
    zt = lax.dot_general(w_ref[...], h, (((1,), (1,)), ((), ())), preferred_element_type=F32)
    nq = N_HEADS * HEAD_DIM
    nkv = N_KV * HEAD_DIM
    cq, sq, ck, sk = cq_ref[...], sq_ref[...], ck_ref[...], sk_ref[...]
    for hd in range(N_HEADS):
        lo = hd * HEAD_DIM
        q_ref[lo:lo + HEAD_DIM, :] = _norm_rope_t(zt[lo:lo + HEAD_DIM], cq, sq).astype(BF16)
    kt = jnp.concatenate(
        [_norm_rope_t(zt[nq + kh * HEAD_DIM:nq + (kh + 1) * HEAD_DIM], ck, sk) for kh in range(N_KV)],
        axis=0)
    k_ref[...] = kt.T.astype(BF16)
    v_ref[...] = zt[nq + nkv:].astype(BF16)


def _rope_tables(q_g, k_g):
    half = HEAD_DIM // 2
    quarter = HEAD_DIM // 4
    pos = jnp.arange(SEQ)
    pos_row = (pos // GRID_W).astype(F32)
    pos_col = (pos % GRID_W).astype(F32)
    inv_freq = ROPE_THETA ** (-jnp.arange(0, half, 2, dtype=F32) / half)
    ang_row = inv_freq[:, None] * pos_row[None, :]
    ang_col = inv_freq[:, None] * pos_col[None, :]
    ang = jnp.concatenate([ang_row, ang_row, ang_col, ang_col], axis=0)
    cos = jnp.concatenate([jnp.cos(ang), jnp.ones((HEAD_DIM, T_QKV), F32)], axis=1)
    sin = jnp.concatenate([jnp.sin(ang), jnp.zeros((HEAD_DIM, T_QKV), F32)], axis=1)
    first = ((jnp.arange(HEAD_DIM) % half) < quarter)[:, None]
    sin = jnp.where(first, -sin, sin)

    def tables(g, scale):
        g = g.astype(F32) * scale
        partner = _swap_halves(g[:, None])
        return g[:, None] * cos, partner * sin

    return tables(q_g, HEAD_DIM ** -0.5 * LOG2_E) + tables(k_g, 1.0)


def _qkv(x, sh, sc, ng, w_qkv, q_g, k_g):
    n = x.shape[0]
    nt = n // T_QKV
    nkv = N_KV * HEAD_DIM
    wd = w_qkv.shape[1]
    per_seq = SEQ // T_QKV
    tab_spec = pl.BlockSpec(
        (HEAD_DIM, T_QKV), lambda i: (0, jnp.where(i < N_LAT // T_QKV, i % per_seq, per_seq)))
    mod = _mod_spec(T_QKV)
    return pl.pallas_call(
        _qkv_body,
        grid=(nt,),
        in_specs=[
            pl.BlockSpec((T_QKV, D), lambda i: (i, 0)),
            mod, mod,
            _const_spec((1, D)),
            _const_spec((wd, D)),
            tab_spec, tab_spec, tab_spec, tab_spec,
        ],
        out_specs=[pl.BlockSpec((D, T_QKV), lambda i: (0, i)),
                   pl.BlockSpec((T_QKV, nkv), lambda i: (i, 0)),
                   pl.BlockSpec((nkv, T_QKV), lambda i: (0, i))],
        out_shape=[jax.ShapeDtypeStruct((D, n), BF16),
                   jax.ShapeDtypeStruct((n, nkv), BF16),
                   jax.ShapeDtypeStruct((nkv, n), BF16)],
        compiler_params=pltpu.CompilerParams(
            dimension_semantics=("arbitrary",), vmem_limit_bytes=VMEM_BIG),
        name="attn_qkv",
    )(x, sh, sc, ng.reshape(1, D), w_qkv.T.astype(BF16), *_rope_tables(q_g, k_g))


def _attn_heads(qt_ref, k_all, vt_all, band, sink_ref, o_ref):
    group = N_HEADS // N_KV
    nk = k_all.shape[0]
    ones = jnp.ones((BF16_ROWS, nk), BF16)
    zeros = jnp.zeros((HEAD_DIM, group * T_Q), BF16)
    outs = []
    scores = []
    probs = []
    for kh in range(N_KV):
        q4 = jnp.concatenate(
            [qt_ref[(kh * group + g) * HEAD_DIM:(kh * group + g + 1) * HEAD_DIM, :] for g in range(group)],
            axis=1)
        qpad = jnp.concatenate([q4, zeros] if kh % 2 == 0 else [zeros, q4], axis=0)
        k2 = k_all[:, (kh // 2) * LANES:(kh // 2 + 1) * LANES]
        scores.append(jnp.dot(k2, qpad, preferred_element_type=F32))
    for kh in range(N_KV):
        st = scores[kh]
        blocks = [st[c * T_Q:(c + 1) * T_Q] for c in range(nk // T_Q)]
        if band is not None:
            blocks[0] = jnp.where(band[0], blocks[0], NEG_INF)
            blocks[2] = jnp.where(band[1], blocks[2], NEG_INF)
        best = blocks[0]
        for blk in blocks[1:]:
            best = jnp.maximum(best, blk)
        sink = jnp.concatenate(
            [jnp.full((1, T_Q), sink_ref[kh * group + g] * LOG2_E, F32) for g in range(group)], axis=1)
        m = jnp.maximum(jnp.max(best, axis=0, keepdims=True), sink)
        pt = jnp.concatenate([jnp.exp2(blk - m).astype(BF16) for blk in blocks], axis=0)
        probs.append((pt, jnp.exp2(sink - m)))
    for kh in range(N_KV):
        pt, sink_term = probs[kh]
        vt_aug = jnp.concatenate([vt_all[kh * HEAD_DIM:(kh + 1) * HEAD_DIM, :], ones], axis=0)
        ot = jnp.dot(vt_aug, pt, preferred_element_type=F32)
        den = ot[HEAD_DIM:HEAD_DIM + 1] + sink_term
        o = ot[:HEAD_DIM] / den
        outs.extend(o[:, g * T_Q:(g + 1) * T_Q] for g in range(group))
    o_ref[...] = jnp.concatenate(outs, axis=0).T.astype(BF16)


def _attn_body(sink_ref, q_ref, kp_ref, kc_ref, kn_ref, kx_ref, vp_ref, vc_ref, vn_ref, vx_ref, o_ref,
               *, n_q_blocks):
    iq = pl.program_id(1)
    per_seq = SEQ // T_Q
    group = N_HEADS // N_KV

    @pl.when(iq < per_seq)
    def _():
        k_all = jnp.concatenate([kp_ref[...], kc_ref[...], kn_ref[...], kx_ref[...]], axis=0)
        vt_all = jnp.concatenate([vp_ref[...], vc_ref[...], vn_ref[...], vx_ref[...]], axis=1)
        c = lax.broadcasted_iota(I32, (T_Q, group * T_Q), 0)
        r = lax.broadcasted_iota(I32, (T_Q, group * T_Q), 1) % T_Q
        far = 2 * T_Q
        mask_prev = c >= r + jnp.where(iq > 0, 0, far)
        mask_next = c <= r - jnp.where(iq < per_seq - 1, 0, far)
        _attn_heads(q_ref, k_all, vt_all, (mask_prev, mask_next), sink_ref, o_ref)

    if n_q_blocks > per_seq:
        @pl.when(iq >= per_seq)
        def _():
            _attn_heads(q_ref, kx_ref[...], vx_ref[...], None, sink_ref, o_ref)


def _attention(qt, k, vt, sink, need_ctx):
    assert T_Q == WINDOW
    per_seq = SEQ // T_Q
    ctx_blocks = CTX_LEN // T_Q
    n_q_blocks = per_seq + (ctx_blocks if need_ctx else 0)
    n_out = N_ALL if need_ctx else N_LAT
    lat_blocks = N_LAT // T_Q
    kw = k.shape[1]

    def q_blk(b, iq):
        return jnp.where(iq < per_seq, b * per_seq + iq, lat_blocks + b * ctx_blocks + (iq - per_seq))

    def win_blk(b, iq, off):
        return b * per_seq + jnp.clip(iq + off, 0, per_seq - 1)

    ctx_blk = lambda b: N_LAT // CTX_LEN + b
    k_win = lambda off: pl.BlockSpec((T_Q, kw), lambda b, iq, *_: (win_blk(b, iq, off), 0))
    v_win = lambda off: pl.BlockSpec((kw, T_Q), lambda b, iq, *_: (0, win_blk(b, iq, off)))
    k_ctx = pl.BlockSpec((CTX_LEN, kw), lambda b, iq, *_: (ctx_blk(b), 0))
    v_ctx = pl.BlockSpec((kw, CTX_LEN), lambda b, iq, *_: (0, ctx_blk(b)))
    grid_spec = pltpu.PrefetchScalarGridSpec(
        num_scalar_prefetch=1,
        grid=(NB, n_q_blocks),
        in_specs=[pl.BlockSpec((D, T_Q), lambda b, iq, *_: (0, q_blk(b, iq))),
                  k_win(-1), k_win(0), k_win(1), k_ctx, v_win(-1), v_win(0), v_win(1), v_ctx],
        out_specs=pl.BlockSpec((T_Q, D), lambda b, iq, *_: (q_blk(b, iq), 0)),
    )
    return pl.pallas_call(
        functools.partial(_attn_body, n_q_blocks=n_q_blocks),
        grid_spec=grid_spec,
        out_shape=jax.ShapeDtypeStruct((n_out, D), BF16),
        compiler_params=pltpu.CompilerParams(
            dimension_semantics=("arbitrary", "arbitrary"), vmem_limit_bytes=VMEM_BIG),
        name="attn_core",
    )(sink.astype(F32), qt, k, k, k, k, vt, vt, vt, vt)


def _proj_body(a_ref, x_ref, gate_ref, w_ref, o_ref):
    y = jnp.dot(a_ref[...], w_ref[...], preferred_element_type=F32)
    o_ref[...] = x_ref[...] + gate_ref[0] * y


def _proj_residual(a, x, gate, w):
    n = a.shape[0]
    return pl.pallas_call(
        _proj_body,
        grid=(n // T_PROJ,),
        in_specs=[pl.BlockSpec((T_PROJ, D), lambda i: (i, 0)),
                  pl.BlockSpec((T_PROJ, D), lambda i: (i, 0)),
                  _mod_spec(T_PROJ),
                  _const_spec((D, D))],
        out_specs=pl.BlockSpec((T_PROJ, D), lambda i: (i, 0)),
        out_shape=jax.ShapeDtypeStruct((n, D), F32),
        compiler_params=pltpu.CompilerParams(dimension_semantics=("arbitrary",)),
        name="attn_proj",
    )(a, x, gate, w.astype(BF16))


def kernel(x, c, ctx, c_ctx, ada_w, ada_b, norm_mix_g, norm_ffn_g, ev_w_in, ev_ln_g, ev_ln_b, ev_ws,
           ev_bs, ev_conv_w, ev_conv_b, ev_cnorm_g, ev_w_out, od_w_qkv, od_q_g, od_k_g, od_sink, od_w_o,
           ff_w1, ff_w3, ff_w2, moe_router, moe_w1, moe_w3, moe_w2):
    assert x.shape == (NB, SEQ, D) and ctx.shape == (NB, CTX_LEN, D)
    mods = _ada_mods(c, c_ctx, ada_w, ada_b)
    xa = jnp.concatenate([x.reshape(N_LAT, D), ctx.reshape(N_CTX, D)], axis=0)
    for li in range(DEPTH):
        need_ctx = li < DEPTH - 1
        j = li // 2
        sh1, sc1, g1, sh2, sc2, g2 = mods[li]
        if li % 2 == 0:
            xa = _even_mixer(xa, sh1, sc1, g1, norm_mix_g[li], ev_w_in[j], ev_ln_g[j], ev_ln_b[j],
                             ev_ws[j], ev_bs[j], ev_conv_w[j], ev_conv_b[j], ev_cnorm_g[j], ev_w_out[j])
            xa = _ffn_dense(xa, sh2, sc2, g2, norm_ffn_g[li], ff_w1[j], ff_w3[j], ff_w2[j])
        else:
            qt, k, vt = _qkv(xa, sh1, sc1, norm_mix_g[li], od_w_qkv[j], od_q_g[j], od_k_g[j])
            o = _attention(qt, k, vt, od_sink[j], need_ctx)
            xa = _proj_residual(o, xa, g1, od_w_o[j])
            n = N_ALL if need_ctx else N_LAT
            chunked = lambda w: w.reshape(N_EXPERTS, D, D_FF // T_FF, T_FF).transpose(0, 2, 1, 3).astype(BF16)
            xa = _moe(xa, n, sh2, sc2, g2, norm_ffn_g[li], moe_router[j],
                      chunked(moe_w1[j]), chunked(moe_w3[j]), moe_w2[j].astype(BF16))
    return xa[:N_LAT].reshape(NB, SEQ, D)
```

```python
import functools

import jax
import jax.numpy as jnp
from jax import lax
from jax.experimental import pallas as pl
from jax.experimental.pallas import tpu as pltpu

F32 = jnp.float32
BF16 = jnp.bfloat16
I32 = jnp.int32

D = 1024
NB = 8
SEQ = 2048
CTX_LEN = 256
DEPTH = 4
GRID_W = 64
CHUNK = 128
A_GROUPS = 8
CONV_W = 31
N_HEADS = 16
N_KV = 4
HEAD_DIM = 64
WINDOW = 128
ROPE_THETA = 10000.0
D_FF = 3584
N_EXPERTS = 8
EPS = 1e-6
NEG_INF = -1e30
LOG2_E = 1.4426950408889634

N_LAT = NB * SEQ
N_CTX = NB * CTX_LEN
N_ALL = N_LAT + N_CTX
CTX_MOD_ROW = NB

LANES = 128
SUBLANES = 8
BF16_ROWS = 16

T_EVEN = 256
EVEN_PAIR = 2
HALO = BF16_ROWS
T_FFN = 1024
T_FF = 512
T_ROUTE = 512
T_MOE = 512
SEG_ALIGN = SUBLANES
S_LOC = 2 * T_ROUTE + N_EXPERTS * SEG_ALIGN
SEG_SIZES = tuple(T_ROUTE >> s for s in range(7))
assert SEG_SIZES[-1] == SEG_ALIGN
T_QKV = 256
T_Q = 128
T_PROJ = 512

VMEM_BIG = 52 * 1024 * 1024


def _mod_spec(tile):
    per = SEQ // tile
    return pl.BlockSpec((1, 1, D), lambda i, *_: (jnp.minimum(i // per, CTX_MOD_ROW), 0, 0))


def _const_spec(shape):
    nd = len(shape)
    return pl.BlockSpec(shape, lambda *_: (0,) * nd)


def _modnorm(x, g, sc, sh):
    ms = jnp.mean(x * x, axis=-1, keepdims=True)
    return (x * lax.rsqrt(ms + EPS) * g) * (1.0 + sc) + sh


def _silu(x):
    return x * jax.nn.sigmoid(x)


def _ada_body(c_ref, w_ref, b_ref, o_ref):
    a = _silu(c_ref[...])
    o_ref[0] = jnp.dot(a.astype(BF16), w_ref[0].astype(BF16), preferred_element_type=F32) + b_ref[0]


def _ada_mods(c, c_ctx, ada_w, ada_b):
    rows = 16
    cc = jnp.concatenate([c, c_ctx[None, :], jnp.zeros((rows - NB - 1, D), F32)], axis=0)
    out = pl.pallas_call(
        _ada_body,
        grid=(DEPTH, 6),
        in_specs=[
            pl.BlockSpec((rows, D), lambda l, n: (0, 0)),
            pl.BlockSpec((1, D, D), lambda l, n: (l, 0, n)),
            pl.BlockSpec((1, 1, D), lambda l, n: (l, 0, n)),
        ],
        out_specs=pl.BlockSpec((1, rows, D), lambda l, n: (l, 0, n)),
        out_shape=jax.ShapeDtypeStruct((DEPTH, rows, 6 * D), F32),
        name="ada_mod",
    )(cc, ada_w, ada_b.reshape(DEPTH, 1, 6 * D))
    m = out[:, :NB + 1].reshape(DEPTH, NB + 1, 6, 1, D)
    return [[m[l, :, k] for k in range(6)] for l in range(DEPTH)]


def _even_body(x_ref, xp_ref, xn_ref, sh_ref, sc_ref, gate_ref, ng_ref, win_ref, lng_ref, lnb_ref,
               ws_ref, bs_ref, cw_ref, cb_ref, cng_ref, wout_ref, o_ref, gext_ref, cv_ref):
    step = pl.program_id(0)
    per_seq = SEQ // T_EVEN
    ext = T_EVEN + 2 * HALO
    halves = range(EVEN_PAIR)
    tiles = [x_ref[hf * T_EVEN:(hf + 1) * T_EVEN] for hf in halves]
    befores = [xp_ref[...]] + [tiles[hf - 1][T_EVEN - HALO:] for hf in halves[1:]]
    afters = [tiles[hf + 1][:HALO] for hf in halves[:-1]] + [xn_ref[...]]

    hs = [_modnorm(jnp.concatenate([befores[hf], tiles[hf], afters[hf]], axis=0),
                   ng_ref[...], sc_ref[0], sh_ref[0]).astype(BF16) for hf in halves]
    zbs = [jnp.dot(hs[hf], win_ref[0, :, 2 * D:], preferred_element_type=F32) for hf in halves]
    zas = [jnp.dot(hs[hf][HALO:HALO + T_EVEN], win_ref[0, :, :2 * D], preferred_element_type=F32)
           for hf in halves]

    for hf in halves:
        i = step * EVEN_PAIR + hf
        is_lat = i < N_LAT // T_EVEN
        pos = i % per_seq
        is_start = jnp.logical_or(jnp.logical_not(is_lat), pos == 0)
        is_end = jnp.logical_or(jnp.logical_not(is_lat), pos == per_seq - 1)
        gg = zbs[hf][:, :D] * jax.nn.sigmoid(zbs[hf][:, D:])
        row = lax.broadcasted_iota(I32, (ext, 1), 0)
        lo = jnp.where(is_start, HALO, 0)
        hi = jnp.where(is_end, T_EVEN + HALO, ext)
        gg = jnp.where(jnp.logical_and(row >= lo, row < hi), gg, 0.0)
        for cbk in range(D // LANES):
            gext_ref[hf, cbk] = gg[:, cbk * LANES:(cbk + 1) * LANES]

    rows_per = 64
    gdim = D // A_GROUPS
    for hf in halves:
        for cbk in range(D // LANES):
            for rb in range(T_EVEN // rows_per):
                acc = jnp.zeros((rows_per, LANES), F32)
                for k in range(CONV_W):
                    off = rb * rows_per + k + HALO - CONV_W // 2
                    acc = acc + cw_ref[cbk, pl.ds(k, 1), :] * gext_ref[hf, cbk, pl.ds(off, rows_per), :]
                cv_ref[hf, cbk, pl.ds(rb * rows_per, rows_per), :] = acc

        cv = jnp.concatenate([cv_ref[hf, cbk] for cbk in range(D // LANES)], axis=1) + cb_ref[...]
        ms = jnp.mean(cv * cv, axis=-1, keepdims=True)
        yb = _silu(cv * lax.rsqrt(ms + EPS) * cng_ref[...])

        u = jax.nn.gelu(zas[hf][:, :D])
        v = jax.nn.gelu(zas[hf][:, D:])
        mu = jnp.mean(v, axis=-1, keepdims=True)
        vc = v - mu
        var = jnp.mean(vc * vc, axis=-1, keepdims=True)
        vn = (vc * lax.rsqrt(var + EPS) * lng_ref[...] + lnb_ref[...]).astype(BF16)
        chunks = []
        for ck in range(T_EVEN // CHUNK):
            blocks = [
                jnp.dot(ws_ref[g], vn[ck * CHUNK:(ck + 1) * CHUNK, g * gdim:(g + 1) * gdim],
                        preferred_element_type=F32)
                for g in range(A_GROUPS)
            ]
            chunks.append(jnp.concatenate(blocks, axis=1) + bs_ref[...])
        ya = u * jnp.concatenate(chunks, axis=0)

        y = (jnp.dot(ya.astype(BF16), wout_ref[0, :D], preferred_element_type=F32)
             + jnp.dot(yb.astype(BF16), wout_ref[0, D:], preferred_element_type=F32))
        o_ref[hf * T_EVEN:(hf + 1) * T_EVEN, :] = tiles[hf] + gate_ref[0] * y


def _even_mixer(x, sh, sc, gate, ng, w_in, ln_g, ln_b, ws, bs, conv_w, conv_b, cn_g, w_out, j):
    n = x.shape[0]
    rows = EVEN_PAIR * T_EVEN
    nt = n // rows
    hb = rows // HALO
    last = n // HALO - 1
    ncb = D // LANES
    bs_full = jnp.repeat(bs.T, D // A_GROUPS, axis=1)
    cw = jnp.pad(conv_w, ((0, 32 - CONV_W), (0, 0))).reshape(32, ncb, LANES).transpose(1, 0, 2)
    mod = _mod_spec(rows)
    row1 = lambda a: a.reshape(1, D)
    return pl.pallas_call(
        _even_body,
        grid=(nt,),
        in_specs=[
            pl.BlockSpec((rows, D), lambda i: (i, 0)),
            pl.BlockSpec((HALO, D), lambda i: (jnp.maximum(i * hb - 1, 0), 0)),
            pl.BlockSpec((HALO, D), lambda i: (jnp.minimum((i + 1) * hb, last), 0)),
            mod, mod, mod,
            _const_spec((1, D)),
            pl.BlockSpec((1, D, 4 * D), lambda i: (j, 0, 0)),
            _const_spec((1, D)), _const_spec((1, D)),
            _const_spec((A_GROUPS, CHUNK, CHUNK)),
            _const_spec((CHUNK, D)),
            _const_spec((ncb, 32, LANES)),
            _const_spec((1, D)), _const_spec((1, D)),
            pl.BlockSpec((1, 2 * D, D), lambda i: (j, 0, 0)),
        ],
        out_specs=pl.BlockSpec((rows, D), lambda i: (i, 0)),
        out_shape=jax.ShapeDtypeStruct((n, D), F32),
        scratch_shapes=[
            pltpu.VMEM((EVEN_PAIR, ncb, T_EVEN + 2 * HALO, LANES), F32),
            pltpu.VMEM((EVEN_PAIR, ncb, T_EVEN, LANES), F32),
        ],
        compiler_params=pltpu.CompilerParams(
            dimension_semantics=("arbitrary",), vmem_limit_bytes=VMEM_BIG),
        name="even_mixer",
    )(x, x, x, sh, sc, gate, row1(ng), w_in, row1(ln_g), row1(ln_b), ws.astype(BF16),
      bs_full, cw, row1(conv_b), row1(cn_g), w_out)


def _ffn_dense_body(x_ref, sh_ref, sc_ref, gate_ref, ng_ref, w1_ref, w3_ref, w2_ref, o_ref,
                    h_ref, acc_ref):
    j = pl.program_id(1)

    @pl.when(j == 0)
    def _():
        h_ref[...] = _modnorm(x_ref[...], ng_ref[...], sc_ref[0], sh_ref[0]).astype(BF16)
        acc_ref[...] = jnp.zeros_like(acc_ref)

    h = h_ref[...]
    a = jnp.dot(h, w1_ref[0].astype(BF16), preferred_element_type=F32)
    b = jnp.dot(h, w3_ref[0].astype(BF16), preferred_element_type=F32)
    t = (_silu(a) * b).astype(BF16)
    acc_ref[...] += jnp.dot(t, w2_ref[0].astype(BF16), preferred_element_type=F32)

    @pl.when(j == pl.num_programs(1) - 1)
    def _():
        o_ref[...] = x_ref[...] + gate_ref[0] * acc_ref[...]


def _ffn_dense(x, sh, sc, gate, ng, w1, w3, w2, layer):
    n = x.shape[0]
    mod = _mod_spec(T_FFN)
    return pl.pallas_call(
        _ffn_dense_body,
        grid=(n // T_FFN, D_FF // T_FF),
        in_specs=[
            pl.BlockSpec((T_FFN, D), lambda i, j: (i, 0)),
            mod, mod, mod,
            _const_spec((1, D)),
            pl.BlockSpec((1, D, T_FF), lambda i, j: (layer, 0, j)),
            pl.BlockSpec((1, D, T_FF), lambda i, j: (layer, 0, j)),
            pl.BlockSpec((1, T_FF, D), lambda i, j: (layer, j, 0)),
        ],
        out_specs=pl.BlockSpec((T_FFN, D), lambda i, j: (i, 0)),
        out_shape=jax.ShapeDtypeStruct((n, D), F32),
        scratch_shapes=[pltpu.VMEM((T_FFN, D), BF16), pltpu.VMEM((T_FFN, D), F32)],
        compiler_params=pltpu.CompilerParams(
            dimension_semantics=("arbitrary", "arbitrary"), vmem_limit_bytes=VMEM_BIG),
        name="ffn_dense",
    )(x, sh, sc, gate, ng.reshape(1, D), w1, w3, w2)


def _router_body(x_ref, sh_ref, sc_ref, ng_ref, rt_ref, s1_ref, s2_ref, w1_ref, w2_ref, pc_ref):
    h = _modnorm(x_ref[...], ng_ref[...], sc_ref[0], sh_ref[0])
    lg = lax.dot_general(rt_ref[...], h, (((1,), (1,)), ((), ())),
                         precision=lax.Precision.HIGHEST, preferred_element_type=F32)
    e = lax.broadcasted_iota(I32, lg.shape, 0).astype(F32)
    big = float(N_EXPERTS)
    m1 = jnp.max(lg, axis=0, keepdims=True)
    i1 = jnp.min(jnp.where(lg == m1, e, big), axis=0, keepdims=True)
    lg2 = jnp.where(e == i1, -jnp.inf, lg)
    m2 = jnp.max(lg2, axis=0, keepdims=True)
    i2 = jnp.min(jnp.where(lg2 == m2, e, big), axis=0, keepdims=True)
    e2 = jnp.exp(m2 - m1)
    den = 1.0 + e2
    w1_ref[0] = 1.0 / den
    w2_ref[0] = e2 / den

    sel1 = e == i1
    sel2 = e == i2
    member = jnp.where(jnp.logical_or(sel1, sel2), 1.0, 0.0)
    t = lg.shape[1]
    before = (lax.broadcasted_iota(I32, (t, t), 0) < lax.broadcasted_iota(I32, (t, t), 1))
    tri = jnp.where(before, 1.0, 0.0).astype(BF16)
    rank = jnp.dot(member.astype(BF16), tri, preferred_element_type=F32)
    count = jnp.sum(member, axis=1, keepdims=True)
    padded = jnp.ceil(count * (1.0 / SEG_ALIGN)) * SEG_ALIGN
    e_col = lax.broadcasted_iota(I32, (N_EXPERTS, 1), 0)
    seg_start = jnp.zeros((N_EXPERTS, 1), F32)
    for ex in range(1, N_EXPERTS):
        below = jnp.sum(jnp.where(e_col < ex, padded, 0.0), axis=0, keepdims=True)
        seg_start = jnp.where(e_col == ex, below, seg_start)
    slot = rank + seg_start
    s1 = jnp.sum(jnp.where(sel1, slot, 0.0), axis=0, keepdims=True).astype(I32)
    s2 = jnp.sum(jnp.where(sel2, slot, 0.0), axis=0, keepdims=True).astype(I32)
    s1_ref[0] = s1
    s2_ref[0] = s2
    pc_ref[0] = jnp.broadcast_to(padded.astype(I32), (N_EXPERTS, LANES))


def _router(x, n, sh, sc, ng, router):
    nt = n // T_ROUTE
    mod = _mod_spec(T_ROUTE)
    vec_spec = pl.BlockSpec((1, 1, T_ROUTE), lambda i: (i, 0, 0))
    vec_i = jax.ShapeDtypeStruct((nt, 1, T_ROUTE), I32)
    vec_f = jax.ShapeDtypeStruct((nt, 1, T_ROUTE), F32)
    return pl.pallas_call(
        _router_body,
        grid=(nt,),
        in_specs=[
            pl.BlockSpec((T_ROUTE, D), lambda i: (i, 0)),
            mod, mod,
            _const_spec((1, D)),
            _const_spec((N_EXPERTS, D)),
        ],
        out_specs=[vec_spec, vec_spec, vec_spec, vec_spec,
                   pl.BlockSpec((1, N_EXPERTS, LANES), lambda i: (i, 0, 0))],
        out_shape=[vec_i, vec_i, vec_f, vec_f, jax.ShapeDtypeStruct((nt, N_EXPERTS, LANES), I32)],
        compiler_params=pltpu.CompilerParams(dimension_semantics=("arbitrary",)),
        name="moe_router",
    )(x, sh, sc, ng.reshape(1, D), router.T)


def _dispatch_body(start_ref, len_ref, tail_start_ref, tail_len_ref, nv_ref,
                   x_ref, sh_ref, sc_ref, ng_ref, s1_ref, s2_ref, xs_ref, lbuf_ref, zbuf_ref, sem,
                   *, n_tiles, min_tiles):
    i = pl.program_id(0)
    slot = i % 2
    other = 1 - slot
    h = _modnorm(x_ref[...], ng_ref[...], sc_ref[0], sh_ref[0]).astype(BF16)
    rows = lax.broadcasted_iota(I32, (S_LOC, T_ROUTE), 0)
    perm = jnp.where(jnp.logical_or(rows == s1_ref[0], rows == s2_ref[0]), 1.0, 0.0).astype(BF16)
    lbuf_ref[slot] = jnp.dot(perm, h, preferred_element_type=F32)

    xs_at = lambda a, p: xs_ref.at[pl.ds(a, p)]
    _segment_copies(_tile_plan(i, len_ref, start_ref), lambda a, p: lbuf_ref.at[slot, pl.ds(a, p)], xs_at,
                    sem.at[slot], start=True)
    _segment_copies(_tile_plan(jnp.maximum(i - 1, 0), len_ref, start_ref),
                    lambda a, p: lbuf_ref.at[other, pl.ds(a, p)], xs_at, sem.at[other], start=False,
                    enable=i > 0)

    @pl.when(i == pl.num_programs(0) - 1)
    def _():
        _segment_copies(_tile_plan(i, len_ref, start_ref), lambda a, p: lbuf_ref.at[slot, pl.ds(a, p)],
                        xs_at, sem.at[slot], start=False)
        zbuf_ref[...] = jnp.zeros_like(zbuf_ref)
        tails = [(tail_len_ref[ex], 0, tail_start_ref[ex]) for ex in range(N_EXPERTS)]
        zero_at = lambda a, p: zbuf_ref.at[pl.ds(a, p)]
        _segment_copies(tails, zero_at, xs_at, sem.at[slot], start=True)
        _segment_copies(tails, zero_at, xs_at, sem.at[slot], start=False)
        for tile in range(min_tiles, n_tiles):
            @pl.when(tile >= nv_ref[0])
            def _(tile=tile):
                cp = pltpu.make_async_copy(zbuf_ref, xs_ref.at[pl.ds(tile * T_MOE, T_MOE)], sem.at[slot])
                cp.start()
                cp.wait()


def _dispatch(x, n, sh, sc, ng, s1, s2, seg_start, seg_len, tail_start, tail_len, n_valid, n_tiles):
    nt = n // T_ROUTE
    mod = _mod_spec(T_ROUTE)
    vec_spec = pl.BlockSpec((1, 1, T_ROUTE), lambda i, *_: (i, 0, 0))
    grid_spec = pltpu.PrefetchScalarGridSpec(
        num_scalar_prefetch=5,
        grid=(nt,),
        in_specs=[pl.BlockSpec((T_ROUTE, D), lambda i, *_: (i, 0)),
                  mod, mod,
                  pl.BlockSpec((1, D), lambda i, *_: (0, 0)),
                  vec_spec, vec_spec],
        out_specs=pl.BlockSpec(memory_space=pl.ANY),
        scratch_shapes=[pltpu.VMEM((2, S_LOC, D), F32), pltpu.VMEM((T_MOE, D), F32),
                        pltpu.SemaphoreType.DMA((2,))],
    )
    return pl.pallas_call(
        functools.partial(_dispatch_body, n_tiles=n_tiles, min_tiles=(2 * n) // T_MOE),
        grid_spec=grid_spec,
        out_shape=jax.ShapeDtypeStruct((n_tiles * T_MOE, D), F32),
        compiler_params=pltpu.CompilerParams(
            dimension_semantics=("arbitrary",), vmem_limit_bytes=VMEM_BIG),
        name="moe_dispatch",
    )(seg_start.reshape(-1), seg_len.reshape(-1), tail_start, tail_len, n_valid,
      x, sh, sc, ng.reshape(1, D), s1, s2)


def _tile_plan(tile, len_ref, far_ref):
    near = 0
    plan = []
    for ex in range(N_EXPERTS):
        ln = len_ref[tile * N_EXPERTS + ex]
        plan.append((ln, near, far_ref[tile * N_EXPERTS + ex]))
        near = near + ln
    return plan


def _segment_copies(plan, src_at, dst_at, sem, start, enable=True):
    for ln, src, dst in plan:
        for p in SEG_SIZES:
            off = jnp.bitwise_and(ln, -(2 * p))

            @pl.when(jnp.logical_and(jnp.bitwise_and(ln, p) != 0, enable))
            def _(off=off, src=src, dst=dst, p=p):
                cp = pltpu.make_async_copy(src_at(pl.multiple_of(src + off, SEG_ALIGN), p),
                                           dst_at(pl.multiple_of(dst + off, SEG_ALIGN), p), sem)
                if start:
                    cp.start()
                else:
                    cp.wait()


def _ffn_moe_body(te_ref, nv_ref, xs_ref, w1_ref, w3_ref, w2_ref, o_ref, xb_ref, acc_ref):
    del te_ref
    i = pl.program_id(0)
    j = pl.program_id(1)
    valid = i < nv_ref[0]

    @pl.when(jnp.logical_and(valid, j == 0))
    def _():
        xb_ref[...] = xs_ref[...].astype(BF16)
        acc_ref[...] = jnp.zeros_like(acc_ref)

    @pl.when(valid)
    def _():
        h = xb_ref[...]
        a = jnp.dot(h, w1_ref[0, 0], preferred_element_type=F32)
        b = jnp.dot(h, w3_ref[0, 0], preferred_element_type=F32)
        t = (_silu(a) * b).astype(BF16)
        acc_ref[...] += jnp.dot(t, w2_ref[0, 0], preferred_element_type=F32)

    last = j == pl.num_programs(1) - 1

    @pl.when(jnp.logical_and(valid, last))
    def _():
        o_ref[...] = acc_ref[...]

    @pl.when(jnp.logical_and(jnp.logical_not(valid), last))
    def _():
        o_ref[...] = jnp.zeros_like(o_ref)


def _ffn_moe(xs, tile_expert, n_valid, w1, w3, w2, layer):
    nj = D_FF // T_FF

    def jj(i, j, nv):
        return jnp.where(i < nv[0], j, nj - 1)

    grid_spec = pltpu.PrefetchScalarGridSpec(
        num_scalar_prefetch=2,
        grid=(xs.shape[0] // T_MOE, nj),
        in_specs=[
            pl.BlockSpec((T_MOE, D), lambda i, j, te, nv: (i, 0)),
            pl.BlockSpec((1, 1, D, T_FF), lambda i, j, te, nv: (layer, te[i], 0, jj(i, j, nv))),
            pl.BlockSpec((1, 1, D, T_FF), lambda i, j, te, nv: (layer, te[i], 0, jj(i, j, nv))),
            pl.BlockSpec((1, 1, T_FF, D), lambda i, j, te, nv: (layer, te[i], jj(i, j, nv), 0)),
        ],
        out_specs=pl.BlockSpec((T_MOE, D), lambda i, j, te, nv: (i, 0)),
        scratch_shapes=[pltpu.VMEM((T_MOE, D), BF16), pltpu.VMEM((T_MOE, D), F32)],
    )
    return pl.pallas_call(
        _ffn_moe_body,
        grid_spec=grid_spec,
        out_shape=jax.ShapeDtypeStruct(xs.shape, F32),
        compiler_params=pltpu.CompilerParams(
            dimension_semantics=("arbitrary", "arbitrary"), vmem_limit_bytes=VMEM_BIG),
        name="ffn_moe",
    )(tile_expert, n_valid, xs, w1, w3, w2)


def _combine_body(start_ref, len_ref, s1_ref, s2_ref, w1_ref, w2_ref, x_ref, gate_ref, os_ref, o_ref,
                  lo_ref, sem):
    i = pl.program_id(0)
    last = pl.num_programs(0) - 1
    slot = i % 2
    other = 1 - slot
    os_at = lambda a, p: os_ref.at[pl.ds(a, p)]

    def gather(tile, buf, start, enable=True):
        plan = [(ln, far, near) for ln, near, far in _tile_plan(tile, len_ref, start_ref)]
        _segment_copies(plan, os_at, lambda a, p: lo_ref.at[buf, pl.ds(a, p)], sem.at[buf], start, enable)

    @pl.when(i == 0)
    def _():
        lo_ref[...] = jnp.zeros_like(lo_ref)
        gather(i, slot, start=True)

    gather(jnp.minimum(i + 1, last), other, start=True, enable=i < last)
    gather(i, slot, start=False)

    lo = lo_ref[slot].astype(BF16)
    lane = lax.broadcasted_iota(I32, (T_ROUTE, S_LOC), 1)
    pick1 = jnp.where(lane == s1_ref[...], 1.0, 0.0).astype(BF16)
    pick2 = jnp.where(lane == s2_ref[...], 1.0, 0.0).astype(BF16)
    y = (w1_ref[...] * jnp.dot(pick1, lo, preferred_element_type=F32)
         + w2_ref[...] * jnp.dot(pick2, lo, preferred_element_type=F32))
    o_ref[...] = x_ref[...] + gate_ref[0] * y


def _combine(x, n, gate, seg_start, seg_len, s1, s2, w1, w2, o_sorted):
    nt = n // T_ROUTE
    col = lambda a: a.reshape(n, 1)
    col_spec = pl.BlockSpec((T_ROUTE, 1), lambda i, *_: (i, 0))
    grid_spec = pltpu.PrefetchScalarGridSpec(
        num_scalar_prefetch=2,
        grid=(nt,),
        in_specs=[col_spec, col_spec, col_spec, col_spec,
                  pl.BlockSpec((T_ROUTE, D), lambda i, *_: (i, 0)),
                  _mod_spec(T_ROUTE),
                  pl.BlockSpec(memory_space=pl.ANY)],
        out_specs=pl.BlockSpec((T_ROUTE, D), lambda i, *_: (i, 0)),
        scratch_shapes=[pltpu.VMEM((2, S_LOC, D), F32), pltpu.SemaphoreType.DMA((2,))],
    )
    return pl.pallas_call(
        _combine_body,
        grid_spec=grid_spec,
        out_shape=jax.ShapeDtypeStruct((n, D), F32),
        compiler_params=pltpu.CompilerParams(
            dimension_semantics=("arbitrary",), vmem_limit_bytes=VMEM_BIG),
        name="moe_combine",
    )(seg_start.reshape(-1), seg_len.reshape(-1), col(s1), col(s2), col(w1), col(w2), x, gate, o_sorted)


def _moe(x, n, sh, sc, gate, ng, router, w1, w3, w2, layer):
    nt = n // T_ROUTE
    s1, s2, p1, p2, pc = _router(x, n, sh, sc, ng, router)
    seg_len = pc[:, :, 0]
    total = jnp.sum(seg_len, axis=0)
    tiles_per = (total + T_MOE - 1) // T_MOE
    tile_ends = jnp.cumsum(tiles_per)
    group_start = (tile_ends - tiles_per) * T_MOE
    seg_start = group_start[None, :] + jnp.cumsum(seg_len, axis=0) - seg_len
    tail_start = group_start + total
    tail_len = tiles_per * T_MOE - total
    n_valid = tile_ends[-1:]
    n_tiles = -(-(2 * n + nt * N_EXPERTS * SEG_ALIGN) // T_MOE) + N_EXPERTS
    tile_ids = jnp.minimum(jnp.arange(n_tiles, dtype=I32), n_valid - 1)
    tile_expert = jnp.sum((tile_ids[:, None] >= tile_ends[None, :]).astype(I32), axis=1)
    xs = _dispatch(x, n, sh, sc, ng, s1, s2, seg_start, seg_len, tail_start, tail_len, n_valid, n_tiles)
    o_sorted = _ffn_moe(xs, tile_expert, n_valid, w1, w3, w2, layer)
    return _combine(x, n, gate, seg_start, seg_len, s1, s2, p1, p2, o_sorted)


def _swap_halves(y):
    q = HEAD_DIM // 4
    return jnp.concatenate([y[q:2 * q], y[:q], y[3 * q:], y[2 * q:3 * q]], axis=0)


def _norm_rope_t(zh, c_tab, s_tab):
    ss = jnp.mean(zh * zh, axis=0, keepdims=True)
    yn = zh * lax.rsqrt(ss + EPS)
    return yn * c_tab + _swap_halves(yn) * s_tab


def _qkv_body(x_ref, sh_ref, sc_ref, ng_ref, w_ref, cq_ref, sq_ref, ck_ref, sk_ref, q_ref, k_ref, v_ref):
    h = _modnorm(x_ref[...], ng_ref[...], sc_ref[0], sh_ref[0]).astype(BF16)
    zt = lax.dot_general(w_ref[...], h, (((1,), (1,)), ((), ())), preferred_element_type=F32)
    nq = N_HEADS * HEAD_DIM
    nkv = N_KV * HEAD_DIM
    cq, sq, ck, sk = cq_ref[...], sq_ref[...], ck_ref[...], sk_ref[...]
    for hd in range(N_HEADS):
        lo = hd * HEAD_DIM
        q_ref[lo:lo + HEAD_DIM, :] = _norm_rope_t(zt[lo:lo + HEAD_DIM], cq, sq).astype(BF16)
    kt = jnp.concatenate(
        [_norm_rope_t(zt[nq + kh * HEAD_DIM:nq + (kh + 1) * HEAD_DIM], ck, sk) for kh in range(N_KV)],
        axis=0)
    k_ref[...] = kt.T.astype(BF16)
    v_ref[...] = zt[nq + nkv:].astype(BF16)


def _rope_tables(q_g, k_g):
    half = HEAD_DIM // 2
    quarter = HEAD_DIM // 4
    pos = jnp.arange(SEQ)
    pos_row = (pos // GRID_W).astype(F32)
    pos_col = (pos % GRID_W).astype(F32)
    inv_freq = ROPE_THETA ** (-jnp.arange(0, half, 2, dtype=F32) / half)
    ang_row = inv_freq[:, None] * pos_row[None, :]
    ang_col = inv_freq[:, None] * pos_col[None, :]
    ang = jnp.concatenate([ang_row, ang_row, ang_col, ang_col], axis=0)
    cos = jnp.concatenate([jnp.cos(ang), jnp.ones((HEAD_DIM, T_QKV), F32)], axis=1)
    sin = jnp.concatenate([jnp.sin(ang), jnp.zeros((HEAD_DIM, T_QKV), F32)], axis=1)
    first = ((jnp.arange(HEAD_DIM) % half) < quarter)[:, None]
    sin = jnp.where(first, -sin, sin)

    def tables(g, scale):
        g = g.astype(F32) * scale
        partner = _swap_halves(g[:, None])
        return g[:, None] * cos, partner * sin

    return tables(q_g, HEAD_DIM ** -0.5 * LOG2_E) + tables(k_g, 1.0)


def _qkv(x, sh, sc, ng, w_qkv, q_g, k_g):
    n = x.shape[0]
    nt = n // T_QKV
    nkv = N_KV * HEAD_DIM
    wd = w_qkv.shape[1]
    per_seq = SEQ // T_QKV
    tab_spec = pl.BlockSpec(
        (HEAD_DIM, T_QKV), lambda i: (0, jnp.where(i < N_LAT // T_QKV, i % per_seq, per_seq)))
    mod = _mod_spec(T_QKV)
    return pl.pallas_call(
        _qkv_body,
        grid=(nt,),
        in_specs=[
            pl.BlockSpec((T_QKV, D), lambda i: (i, 0)),
            mod, mod,
            _const_spec((1, D)),
            _const_spec((wd, D)),
            tab_spec, tab_spec, tab_spec, tab_spec,
        ],
        out_specs=[pl.BlockSpec((D, T_QKV), lambda i: (0, i)),
                   pl.BlockSpec((T_QKV, nkv), lambda i: (i, 0)),
                   pl.BlockSpec((nkv, T_QKV), lambda i: (0, i))],
        out_shape=[jax.ShapeDtypeStruct((D, n), BF16),
                   jax.ShapeDtypeStruct((n, nkv), BF16),
                   jax.ShapeDtypeStruct((nkv, n), BF16)],
        compiler_params=pltpu.CompilerParams(
            dimension_semantics=("arbitrary",), vmem_limit_bytes=VMEM_BIG),
        name="attn_qkv",
    )(x, sh, sc, ng.reshape(1, D), w_qkv.T.astype(BF16), *_rope_tables(q_g, k_g))


def _attn_heads(qt_ref, k_all, vt_all, band, sink_ref, o_ref):
    group = N_HEADS // N_KV
    nk = k_all.shape[0]
    ones = jnp.ones((BF16_ROWS, nk), BF16)
    zeros = jnp.zeros((HEAD_DIM, group * T_Q), BF16)
    outs = []
    scores = []
    probs = []
    for kh in range(N_KV):
        q4 = jnp.concatenate(
            [qt_ref[(kh * group + g) * HEAD_DIM:(kh * group + g + 1) * HEAD_DIM, :] for g in range(group)],
            axis=1)
        qpad = jnp.concatenate([q4, zeros] if kh % 2 == 0 else [zeros, q4], axis=0)
        k2 = k_all[:, (kh // 2) * LANES:(kh // 2 + 1) * LANES]
        scores.append(jnp.dot(k2, qpad, preferred_element_type=F32))
    for kh in range(N_KV):
        st = scores[kh]
        blocks = [st[c * T_Q:(c + 1) * T_Q] for c in range(nk // T_Q)]
        if band is not None:
            blocks[0] = jnp.where(band[0], blocks[0], NEG_INF)
            blocks[2] = jnp.where(band[1], blocks[2], NEG_INF)
        best = blocks[0]
        for blk in blocks[1:]:
            best = jnp.maximum(best, blk)
        sink = jnp.concatenate(
            [jnp.full((1, T_Q), sink_ref[kh * group + g] * LOG2_E, F32) for g in range(group)], axis=1)
        m = jnp.maximum(jnp.max(best, axis=0, keepdims=True), sink)
        pt = jnp.concatenate([jnp.exp2(blk - m).astype(BF16) for blk in blocks], axis=0)
        probs.append((pt, jnp.exp2(sink - m)))
    for kh in range(N_KV):
        pt, sink_term = probs[kh]
        vt_aug = jnp.concatenate([vt_all[kh * HEAD_DIM:(kh + 1) * HEAD_DIM, :], ones], axis=0)
        ot = jnp.dot(vt_aug, pt, preferred_element_type=F32)
        den = ot[HEAD_DIM:HEAD_DIM + 1] + sink_term
        o = ot[:HEAD_DIM] / den
        outs.extend(o[:, g * T_Q:(g + 1) * T_Q] for g in range(group))
    o_ref[...] = jnp.concatenate(outs, axis=0).T.astype(BF16)


def _attn_body(sink_ref, q_ref, kp_ref, kc_ref, kn_ref, kx_ref, vp_ref, vc_ref, vn_ref, vx_ref, o_ref,
               *, n_q_blocks):
    iq = pl.program_id(1)
    per_seq = SEQ // T_Q
    group = N_HEADS // N_KV

    @pl.when(iq < per_seq)
    def _():
        k_all = jnp.concatenate([kp_ref[...], kc_ref[...], kn_ref[...], kx_ref[...]], axis=0)
        vt_all = jnp.concatenate([vp_ref[...], vc_ref[...], vn_ref[...], vx_ref[...]], axis=1)
        c = lax.broadcasted_iota(I32, (T_Q, group * T_Q), 0)
        r = lax.broadcasted_iota(I32, (T_Q, group * T_Q), 1) % T_Q
        far = 2 * T_Q
        mask_prev = c >= r + jnp.where(iq > 0, 0, far)
        mask_next = c <= r - jnp.where(iq < per_seq - 1, 0, far)
        _attn_heads(q_ref, k_all, vt_all, (mask_prev, mask_next), sink_ref, o_ref)

    if n_q_blocks > per_seq:
        @pl.when(iq >= per_seq)
        def _():
            _attn_heads(q_ref, kx_ref[...], vx_ref[...], None, sink_ref, o_ref)


def _attention(qt, k, vt, sink, need_ctx):
    assert T_Q == WINDOW
    per_seq = SEQ // T_Q
    ctx_blocks = CTX_LEN // T_Q
    n_q_blocks = per_seq + (ctx_blocks if need_ctx else 0)
    n_out = N_ALL if need_ctx else N_LAT
    lat_blocks = N_LAT // T_Q
    kw = k.shape[1]

    def q_blk(b, iq):
        return jnp.where(iq < per_seq, b * per_seq + iq, lat_blocks + b * ctx_blocks + (iq - per_seq))

    def win_blk(b, iq, off):
        return b * per_seq + jnp.clip(iq + off, 0, per_seq - 1)

    ctx_blk = lambda b: N_LAT // CTX_LEN + b
    k_win = lambda off: pl.BlockSpec((T_Q, kw), lambda b, iq, *_: (win_blk(b, iq, off), 0))
    v_win = lambda off: pl.BlockSpec((kw, T_Q), lambda b, iq, *_: (0, win_blk(b, iq, off)))
    k_ctx = pl.BlockSpec((CTX_LEN, kw), lambda b, iq, *_: (ctx_blk(b), 0))
    v_ctx = pl.BlockSpec((kw, CTX_LEN), lambda b, iq, *_: (0, ctx_blk(b)))
    grid_spec = pltpu.PrefetchScalarGridSpec(
        num_scalar_prefetch=1,
        grid=(NB, n_q_blocks),
        in_specs=[pl.BlockSpec((D, T_Q), lambda b, iq, *_: (0, q_blk(b, iq))),
                  k_win(-1), k_win(0), k_win(1), k_ctx, v_win(-1), v_win(0), v_win(1), v_ctx],
        out_specs=pl.BlockSpec((T_Q, D), lambda b, iq, *_: (q_blk(b, iq), 0)),
    )
    return pl.pallas_call(
        functools.partial(_attn_body, n_q_blocks=n_q_blocks),
        grid_spec=grid_spec,
        out_shape=jax.ShapeDtypeStruct((n_out, D), BF16),
        compiler_params=pltpu.CompilerParams(
            dimension_semantics=("arbitrary", "arbitrary"), vmem_limit_bytes=VMEM_BIG),
        name="attn_core",
    )(sink.astype(F32), qt, k, k, k, k, vt, vt, vt, vt)


def _proj_body(a_ref, x_ref, gate_ref, w_ref, o_ref):
    y = jnp.dot(a_ref[...], w_ref[...], preferred_element_type=F32)
    o_ref[...] = x_ref[...] + gate_ref[0] * y


def _proj_residual(a, x, gate, w):
    n = a.shape[0]
    return pl.pallas_call(
        _proj_body,
        grid=(n // T_PROJ,),
        in_specs=[pl.BlockSpec((T_PROJ, D), lambda i: (i, 0)),
                  pl.BlockSpec((T_PROJ, D), lambda i: (i, 0)),
                  _mod_spec(T_PROJ),
                  _const_spec((D, D))],
        out_specs=pl.BlockSpec((T_PROJ, D), lambda i: (i, 0)),
        out_shape=jax.ShapeDtypeStruct((n, D), F32),
        compiler_params=pltpu.CompilerParams(dimension_semantics=("arbitrary",)),
        name="attn_proj",
    )(a, x, gate, w.astype(BF16))


def kernel(x, c, ctx, c_ctx, ada_w, ada_b, norm_mix_g, norm_ffn_g, ev_w_in, ev_ln_g, ev_ln_b, ev_ws,
           ev_bs, ev_conv_w, ev_conv_b, ev_cnorm_g, ev_w_out, od_w_qkv, od_q_g, od_k_g, od_sink, od_w_o,
           ff_w1, ff_w3, ff_w2, moe_router, moe_w1, moe_w3, moe_w2):
    assert x.shape == (NB, SEQ, D) and ctx.shape == (NB, CTX_LEN, D)
    mods = _ada_mods(c, c_ctx, ada_w, ada_b)
    xa = jnp.concatenate([x.reshape(N_LAT, D), ctx.reshape(N_CTX, D)], axis=0)
    ev_w_in_b, ev_w_out_b = ev_w_in.astype(BF16), ev_w_out.astype(BF16)
    moe_w1_b, moe_w3_b, moe_w2_b = moe_w1.astype(BF16), moe_w3.astype(BF16), moe_w2.astype(BF16)
    for li in range(DEPTH):
        need_ctx = li < DEPTH - 1
        j = li // 2
        sh1, sc1, g1, sh2, sc2, g2 = mods[li]
        if li % 2 == 0:
            xa = _even_mixer(xa, sh1, sc1, g1, norm_mix_g[li], ev_w_in_b, ev_ln_g[j], ev_ln_b[j],
                             ev_ws[j], ev_bs[j], ev_conv_w[j], ev_conv_b[j], ev_cnorm_g[j], ev_w_out_b, j)
            xa = _ffn_dense(xa, sh2, sc2, g2, norm_ffn_g[li], ff_w1, ff_w3, ff_w2, j)
        else:
            qt, k, vt = _qkv(xa, sh1, sc1, norm_mix_g[li], od_w_qkv[j], od_q_g[j], od_k_g[j])
            o = _attention(qt, k, vt, od_sink[j], need_ctx)
            xa = _proj_residual(o, xa, g1, od_w_o[j])
            n = N_ALL if need_ctx else N_LAT
            xa = _moe(xa, n, sh2, sc2, g2, norm_ffn_g[li], moe_router[j], moe_w1_b, moe_w3_b, moe_w2_b, j)
    return xa[:N_LAT].reshape(NB, SEQ, D)
```

```python
import functools

import jax
import jax.numpy as jnp
from jax import lax
from jax.experimental import pallas as pl
from jax.experimental.pallas import tpu as pltpu

F32 = jnp.float32
BF16 = jnp.bfloat16
I32 = jnp.int32

D = 1024
NB = 8
SEQ = 2048
CTX_LEN = 256
DEPTH = 4
GRID_W = 64
CHUNK = 128
A_GROUPS = 8
CONV_W = 31
N_HEADS = 16
N_KV = 4
HEAD_DIM = 64
WINDOW = 128
ROPE_THETA = 10000.0
D_FF = 3584
N_EXPERTS = 8
EPS = 1e-6
NEG_INF = -1e30
LOG2_E = 1.4426950408889634

N_LAT = NB * SEQ
N_CTX = NB * CTX_LEN
N_ALL = N_LAT + N_CTX
CTX_MOD_ROW = NB

LANES = 128
SUBLANES = 8
BF16_ROWS = 16

T_EVEN = 256
EVEN_PAIR = 2
HALO = BF16_ROWS
T_FFN = 1024
T_FF = 512
T_ROUTE = 512
T_MOE = 1024
SEG_ALIGN = SUBLANES
S_LOC = 2 * T_ROUTE + N_EXPERTS * SEG_ALIGN
SEG_SIZES = tuple(T_ROUTE >> s for s in range(7))
assert SEG_SIZES[-1] == SEG_ALIGN
T_QKV = 256
T_Q = 128
T_PROJ = 512

VMEM_BIG = 52 * 1024 * 1024


def _mod_spec(tile):
    per = SEQ // tile
    return pl.BlockSpec((1, 1, D), lambda i, *_: (jnp.minimum(i // per, CTX_MOD_ROW), 0, 0))


def _const_spec(shape):
    nd = len(shape)
    return pl.BlockSpec(shape, lambda *_: (0,) * nd)


def _modnorm(x, g, sc, sh):
    ms = jnp.mean(x * x, axis=-1, keepdims=True)
    return (x * lax.rsqrt(ms + EPS) * g) * (1.0 + sc) + sh


def _silu(x):
    return x * jax.nn.sigmoid(x)


def _ada_body(c_ref, w_ref, b_ref, o_ref):
    a = _silu(c_ref[...])
    o_ref[0] = jnp.dot(a.astype(BF16), w_ref[0].astype(BF16), preferred_element_type=F32) + b_ref[0]


def _ada_mods(c, c_ctx, ada_w, ada_b):
    rows = 16
    cc = jnp.concatenate([c, c_ctx[None, :], jnp.zeros((rows - NB - 1, D), F32)], axis=0)
    out = pl.pallas_call(
        _ada_body,
        grid=(DEPTH, 6),
        in_specs=[
            pl.BlockSpec((rows, D), lambda l, n: (0, 0)),
            pl.BlockSpec((1, D, D), lambda l, n: (l, 0, n)),
            pl.BlockSpec((1, 1, D), lambda l, n: (l, 0, n)),
        ],
        out_specs=pl.BlockSpec((1, rows, D), lambda l, n: (l, 0, n)),
        out_shape=jax.ShapeDtypeStruct((DEPTH, rows, 6 * D), F32),
        name="ada_mod",
    )(cc, ada_w, ada_b.reshape(DEPTH, 1, 6 * D))
    m = out[:, :NB + 1].reshape(DEPTH, NB + 1, 6, 1, D)
    return [[m[l, :, k] for k in range(6)] for l in range(DEPTH)]


def _even_body(x_ref, xp_ref, xn_ref, sh_ref, sc_ref, gate_ref, ng_ref, win_ref, lng_ref, lnb_ref,
               ws_ref, bs_ref, cw_ref, cb_ref, cng_ref, wout_ref, o_ref, gext_ref, cv_ref):
    step = pl.program_id(0)
    per_seq = SEQ // T_EVEN
    ext = T_EVEN + 2 * HALO
    halves = range(EVEN_PAIR)
    tiles = [x_ref[hf * T_EVEN:(hf + 1) * T_EVEN] for hf in halves]
    befores = [xp_ref[...]] + [tiles[hf - 1][T_EVEN - HALO:] for hf in halves[1:]]
    afters = [tiles[hf + 1][:HALO] for hf in halves[:-1]] + [xn_ref[...]]

    hs = [_modnorm(jnp.concatenate([befores[hf], tiles[hf], afters[hf]], axis=0),
                   ng_ref[...], sc_ref[0], sh_ref[0]).astype(BF16) for hf in halves]
    zbs = [jnp.dot(hs[hf], win_ref[0, :, 2 * D:], preferred_element_type=F32) for hf in halves]
    zas = [jnp.dot(hs[hf][HALO:HALO + T_EVEN], win_ref[0, :, :2 * D], preferred_element_type=F32)
           for hf in halves]

    for hf in halves:
        i = step * EVEN_PAIR + hf
        is_lat = i < N_LAT // T_EVEN
        pos = i % per_seq
        is_start = jnp.logical_or(jnp.logical_not(is_lat), pos == 0)
        is_end = jnp.logical_or(jnp.logical_not(is_lat), pos == per_seq - 1)
        gg = zbs[hf][:, :D] * jax.nn.sigmoid(zbs[hf][:, D:])
        row = lax.broadcasted_iota(I32, (ext, 1), 0)
        lo = jnp.where(is_start, HALO, 0)
        hi = jnp.where(is_end, T_EVEN + HALO, ext)
        gg = jnp.where(jnp.logical_and(row >= lo, row < hi), gg, 0.0)
        for cbk in range(D // LANES):
            gext_ref[hf, cbk] = gg[:, cbk * LANES:(cbk + 1) * LANES]

    rows_per = 64
    gdim = D // A_GROUPS
    for hf in halves:
        for cbk in range(D // LANES):
            for rb in range(T_EVEN // rows_per):
                acc = jnp.zeros((rows_per, LANES), F32)
                for k in range(CONV_W):
                    off = rb * rows_per + k + HALO - CONV_W // 2
                    acc = acc + cw_ref[cbk, pl.ds(k, 1), :] * gext_ref[hf, cbk, pl.ds(off, rows_per), :]
                cv_ref[hf, cbk, pl.ds(rb * rows_per, rows_per), :] = acc

        cv = jnp.concatenate([cv_ref[hf, cbk] for cbk in range(D // LANES)], axis=1) + cb_ref[...]
        ms = jnp.mean(cv * cv, axis=-1, keepdims=True)
        yb = _silu(cv * lax.rsqrt(ms + EPS) * cng_ref[...])

        u = jax.nn.gelu(zas[hf][:, :D])
        v = jax.nn.gelu(zas[hf][:, D:])
        mu = jnp.mean(v, axis=-1, keepdims=True)
        vc = v - mu
        var = jnp.mean(vc * vc, axis=-1, keepdims=True)
        vn = (vc * lax.rsqrt(var + EPS) * lng_ref[...] + lnb_ref[...]).astype(BF16)
        chunks = []
        for ck in range(T_EVEN // CHUNK):
            blocks = [
                jnp.dot(ws_ref[g], vn[ck * CHUNK:(ck + 1) * CHUNK, g * gdim:(g + 1) * gdim],
                        preferred_element_type=F32)
                for g in range(A_GROUPS)
            ]
            chunks.append(jnp.concatenate(blocks, axis=1) + bs_ref[...])
        ya = u * jnp.concatenate(chunks, axis=0)

        y = (jnp.dot(ya.astype(BF16), wout_ref[0, :D], preferred_element_type=F32)
             + jnp.dot(yb.astype(BF16), wout_ref[0, D:], preferred_element_type=F32))
        o_ref[hf * T_EVEN:(hf + 1) * T_EVEN, :] = tiles[hf] + gate_ref[0] * y


def _even_mixer(x, sh, sc, gate, ng, w_in, ln_g, ln_b, ws, bs, conv_w, conv_b, cn_g, w_out, j):
    n = x.shape[0]
    rows = EVEN_PAIR * T_EVEN
    nt = n // rows
    hb = rows // HALO
    last = n // HALO - 1
    ncb = D // LANES
    bs_full = jnp.repeat(bs.T, D // A_GROUPS, axis=1)
    cw = jnp.pad(conv_w, ((0, 32 - CONV_W), (0, 0))).reshape(32, ncb, LANES).transpose(1, 0, 2)
    mod = _mod_spec(rows)
    row1 = lambda a: a.reshape(1, D)
    return pl.pallas_call(
        _even_body,
        grid=(nt,),
        in_specs=[
            pl.BlockSpec((rows, D), lambda i: (i, 0)),
            pl.BlockSpec((HALO, D), lambda i: (jnp.maximum(i * hb - 1, 0), 0)),
            pl.BlockSpec((HALO, D), lambda i: (jnp.minimum((i + 1) * hb, last), 0)),
            mod, mod, mod,
            _const_spec((1, D)),
            pl.BlockSpec((1, D, 4 * D), lambda i: (j, 0, 0)),
            _const_spec((1, D)), _const_spec((1, D)),
            _const_spec((A_GROUPS, CHUNK, CHUNK)),
            _const_spec((CHUNK, D)),
            _const_spec((ncb, 32, LANES)),
            _const_spec((1, D)), _const_spec((1, D)),
            pl.BlockSpec((1, 2 * D, D), lambda i: (j, 0, 0)),
        ],
        out_specs=pl.BlockSpec((rows, D), lambda i: (i, 0)),
        out_shape=jax.ShapeDtypeStruct((n, D), F32),
        scratch_shapes=[
            pltpu.VMEM((EVEN_PAIR, ncb, T_EVEN + 2 * HALO, LANES), F32),
            pltpu.VMEM((EVEN_PAIR, ncb, T_EVEN, LANES), F32),
        ],
        compiler_params=pltpu.CompilerParams(
            dimension_semantics=("arbitrary",), vmem_limit_bytes=VMEM_BIG),
        name="even_mixer",
    )(x, x, x, sh, sc, gate, row1(ng), w_in, row1(ln_g), row1(ln_b), ws.astype(BF16),
      bs_full, cw, row1(conv_b), row1(cn_g), w_out)


def _ffn_dense_body(x_ref, sh_ref, sc_ref, gate_ref, ng_ref, w1_ref, w3_ref, w2_ref, o_ref,
                    h_ref, acc_ref):
    j = pl.program_id(1)

    @pl.when(j == 0)
    def _():
        h_ref[...] = _modnorm(x_ref[...], ng_ref[...], sc_ref[0], sh_ref[0]).astype(BF16)
        acc_ref[...] = jnp.zeros_like(acc_ref)

    h = h_ref[...]
    a = jnp.dot(h, w1_ref[0].astype(BF16), preferred_element_type=F32)
    b = jnp.dot(h, w3_ref[0].astype(BF16), preferred_element_type=F32)
    t = (_silu(a) * b).astype(BF16)
    acc_ref[...] += jnp.dot(t, w2_ref[0].astype(BF16), preferred_element_type=F32)

    @pl.when(j == pl.num_programs(1) - 1)
    def _():
        o_ref[...] = x_ref[...] + gate_ref[0] * acc_ref[...]


def _ffn_dense(x, sh, sc, gate, ng, w1, w3, w2, layer):
    n = x.shape[0]
    mod = _mod_spec(T_FFN)
    return pl.pallas_call(
        _ffn_dense_body,
        grid=(n // T_FFN, D_FF // T_FF),
        in_specs=[
            pl.BlockSpec((T_FFN, D), lambda i, j: (i, 0)),
            mod, mod, mod,
            _const_spec((1, D)),
            pl.BlockSpec((1, D, T_FF), lambda i, j: (layer, 0, j)),
            pl.BlockSpec((1, D, T_FF), lambda i, j: (layer, 0, j)),
            pl.BlockSpec((1, T_FF, D), lambda i, j: (layer, j, 0)),
        ],
        out_specs=pl.BlockSpec((T_FFN, D), lambda i, j: (i, 0)),
        out_shape=jax.ShapeDtypeStruct((n, D), F32),
        scratch_shapes=[pltpu.VMEM((T_FFN, D), BF16), pltpu.VMEM((T_FFN, D), F32)],
        compiler_params=pltpu.CompilerParams(
            dimension_semantics=("arbitrary", "arbitrary"), vmem_limit_bytes=VMEM_BIG),
        name="ffn_dense",
    )(x, sh, sc, gate, ng.reshape(1, D), w1, w3, w2)


def _router_body(x_ref, sh_ref, sc_ref, ng_ref, rt_ref, s1_ref, s2_ref, w1_ref, w2_ref, pc_ref):
    h = _modnorm(x_ref[...], ng_ref[...], sc_ref[0], sh_ref[0])
    lg = lax.dot_general(rt_ref[...], h, (((1,), (1,)), ((), ())),
                         precision=lax.Precision.HIGHEST, preferred_element_type=F32)
    e = lax.broadcasted_iota(I32, lg.shape, 0).astype(F32)
    big = float(N_EXPERTS)
    m1 = jnp.max(lg, axis=0, keepdims=True)
    i1 = jnp.min(jnp.where(lg == m1, e, big), axis=0, keepdims=True)
    lg2 = jnp.where(e == i1, -jnp.inf, lg)
    m2 = jnp.max(lg2, axis=0, keepdims=True)
    i2 = jnp.min(jnp.where(lg2 == m2, e, big), axis=0, keepdims=True)
    e2 = jnp.exp(m2 - m1)
    den = 1.0 + e2
    w1_ref[0] = 1.0 / den
    w2_ref[0] = e2 / den

    sel1 = e == i1
    sel2 = e == i2
    member = jnp.where(jnp.logical_or(sel1, sel2), 1.0, 0.0)
    t = lg.shape[1]
    before = (lax.broadcasted_iota(I32, (t, t), 0) < lax.broadcasted_iota(I32, (t, t), 1))
    tri = jnp.where(before, 1.0, 0.0).astype(BF16)
    rank = jnp.dot(member.astype(BF16), tri, preferred_element_type=F32)
    count = jnp.sum(member, axis=1, keepdims=True)
    padded = jnp.ceil(count * (1.0 / SEG_ALIGN)) * SEG_ALIGN
    e_col = lax.broadcasted_iota(I32, (N_EXPERTS, 1), 0)
    seg_start = jnp.zeros((N_EXPERTS, 1), F32)
    for ex in range(1, N_EXPERTS):
        below = jnp.sum(jnp.where(e_col < ex, padded, 0.0), axis=0, keepdims=True)
        seg_start = jnp.where(e_col == ex, below, seg_start)
    slot = rank + seg_start
    s1 = jnp.sum(jnp.where(sel1, slot, 0.0), axis=0, keepdims=True).astype(I32)
    s2 = jnp.sum(jnp.where(sel2, slot, 0.0), axis=0, keepdims=True).astype(I32)
    s1_ref[0] = s1
    s2_ref[0] = s2
    pc_ref[0] = jnp.broadcast_to(padded.astype(I32), (N_EXPERTS, LANES))


def _router(x, n, sh, sc, ng, router):
    nt = n // T_ROUTE
    mod = _mod_spec(T_ROUTE)
    vec_spec = pl.BlockSpec((1, 1, T_ROUTE), lambda i: (i, 0, 0))
    vec_i = jax.ShapeDtypeStruct((nt, 1, T_ROUTE), I32)
    vec_f = jax.ShapeDtypeStruct((nt, 1, T_ROUTE), F32)
    return pl.pallas_call(
        _router_body,
        grid=(nt,),
        in_specs=[
            pl.BlockSpec((T_ROUTE, D), lambda i: (i, 0)),
            mod, mod,
            _const_spec((1, D)),
            _const_spec((N_EXPERTS, D)),
        ],
        out_specs=[vec_spec, vec_spec, vec_spec, vec_spec,
                   pl.BlockSpec((1, N_EXPERTS, LANES), lambda i: (i, 0, 0))],
        out_shape=[vec_i, vec_i, vec_f, vec_f, jax.ShapeDtypeStruct((nt, N_EXPERTS, LANES), I32)],
        compiler_params=pltpu.CompilerParams(dimension_semantics=("arbitrary",)),
        name="moe_router",
    )(x, sh, sc, ng.reshape(1, D), router.T)


def _dispatch_body(start_ref, len_ref, tail_start_ref, tail_len_ref, nv_ref,
                   x_ref, sh_ref, sc_ref, ng_ref, s1_ref, s2_ref, xs_ref, lbuf_ref, zbuf_ref, sem,
                   *, n_tiles, min_tiles):
    i = pl.program_id(0)
    slot = i % 2
    other = 1 - slot
    h = _modnorm(x_ref[...], ng_ref[...], sc_ref[0], sh_ref[0]).astype(BF16)
    rows = lax.broadcasted_iota(I32, (S_LOC, T_ROUTE), 0)
    perm = jnp.where(jnp.logical_or(rows == s1_ref[0], rows == s2_ref[0]), 1.0, 0.0).astype(BF16)
    lbuf_ref[slot] = jnp.dot(perm, h, preferred_element_type=F32)

    xs_at = lambda a, p: xs_ref.at[pl.ds(a, p)]
    _segment_copies(_tile_plan(i, len_ref, start_ref), lambda a, p: lbuf_ref.at[slot, pl.ds(a, p)], xs_at,
                    sem.at[slot], start=True)
    _segment_copies(_tile_plan(jnp.maximum(i - 1, 0), len_ref, start_ref),
                    lambda a, p: lbuf_ref.at[other, pl.ds(a, p)], xs_at, sem.at[other], start=False,
                    enable=i > 0)

    @pl.when(i == pl.num_programs(0) - 1)
    def _():
        _segment_copies(_tile_plan(i, len_ref, start_ref), lambda a, p: lbuf_ref.at[slot, pl.ds(a, p)],
                        xs_at, sem.at[slot], start=False)
        zbuf_ref[...] = jnp.zeros_like(zbuf_ref)
        tails = [(tail_len_ref[ex], 0, tail_start_ref[ex]) for ex in range(N_EXPERTS)]
        zero_at = lambda a, p: zbuf_ref.at[pl.ds(a, p)]
        _segment_copies(tails, zero_at, xs_at, sem.at[slot], start=True)
        _segment_copies(tails, zero_at, xs_at, sem.at[slot], start=False)
        for tile in range(min_tiles, n_tiles):
            @pl.when(tile >= nv_ref[0])
            def _(tile=tile):
                cp = pltpu.make_async_copy(zbuf_ref, xs_ref.at[pl.ds(tile * T_MOE, T_MOE)], sem.at[slot])
                cp.start()
                cp.wait()


def _dispatch(x, n, sh, sc, ng, s1, s2, seg_start, seg_len, tail_start, tail_len, n_valid, n_tiles):
    nt = n // T_ROUTE
    mod = _mod_spec(T_ROUTE)
    vec_spec = pl.BlockSpec((1, 1, T_ROUTE), lambda i, *_: (i, 0, 0))
    grid_spec = pltpu.PrefetchScalarGridSpec(
        num_scalar_prefetch=5,
        grid=(nt,),
        in_specs=[pl.BlockSpec((T_ROUTE, D), lambda i, *_: (i, 0)),
                  mod, mod,
                  pl.BlockSpec((1, D), lambda i, *_: (0, 0)),
                  vec_spec, vec_spec],
        out_specs=pl.BlockSpec(memory_space=pl.ANY),
        scratch_shapes=[pltpu.VMEM((2, S_LOC, D), F32), pltpu.VMEM((T_MOE, D), F32),
                        pltpu.SemaphoreType.DMA((2,))],
    )
    return pl.pallas_call(
        functools.partial(_dispatch_body, n_tiles=n_tiles, min_tiles=(2 * n) // T_MOE),
        grid_spec=grid_spec,
        out_shape=jax.ShapeDtypeStruct((n_tiles * T_MOE, D), F32),
        compiler_params=pltpu.CompilerParams(
            dimension_semantics=("arbitrary",), vmem_limit_bytes=VMEM_BIG),
        name="moe_dispatch",
    )(seg_start.reshape(-1), seg_len.reshape(-1), tail_start, tail_len, n_valid,
      x, sh, sc, ng.reshape(1, D), s1, s2)


def _tile_plan(tile, len_ref, far_ref):
    near = 0
    plan = []
    for ex in range(N_EXPERTS):
        ln = len_ref[tile * N_EXPERTS + ex]
        plan.append((ln, near, far_ref[tile * N_EXPERTS + ex]))
        near = near + ln
    return plan


def _segment_copies(plan, src_at, dst_at, sem, start, enable=True):
    for ln, src, dst in plan:
        for p in SEG_SIZES:
            off = jnp.bitwise_and(ln, -(2 * p))

            @pl.when(jnp.logical_and(jnp.bitwise_and(ln, p) != 0, enable))
            def _(off=off, src=src, dst=dst, p=p):
                cp = pltpu.make_async_copy(src_at(pl.multiple_of(src + off, SEG_ALIGN), p),
                                           dst_at(pl.multiple_of(dst + off, SEG_ALIGN), p), sem)
                if start:
                    cp.start()
                else:
                    cp.wait()


def _ffn_moe_body(te_ref, nv_ref, xs_ref, w1_ref, w3_ref, w2_ref, o_ref, xb_ref, acc_ref):
    del te_ref
    i = pl.program_id(0)
    j = pl.program_id(1)
    valid = i < nv_ref[0]

    @pl.when(jnp.logical_and(valid, j == 0))
    def _():
        xb_ref[...] = xs_ref[...].astype(BF16)
        acc_ref[...] = jnp.zeros_like(acc_ref)

    @pl.when(valid)
    def _():
        h = xb_ref[...]
        a = jnp.dot(h, w1_ref[0, 0].astype(BF16), preferred_element_type=F32)
        b = jnp.dot(h, w3_ref[0, 0].astype(BF16), preferred_element_type=F32)
        t = (_silu(a) * b).astype(BF16)
        acc_ref[...] += jnp.dot(t, w2_ref[0, 0].astype(BF16), preferred_element_type=F32)

    last = j == pl.num_programs(1) - 1

    @pl.when(jnp.logical_and(valid, last))
    def _():
        o_ref[...] = acc_ref[...]

    @pl.when(jnp.logical_and(jnp.logical_not(valid), last))
    def _():
        o_ref[...] = jnp.zeros_like(o_ref)


def _ffn_moe(xs, tile_expert, n_valid, w1, w3, w2, layer):
    nj = D_FF // T_FF

    def jj(i, j, nv):
        return jnp.where(i < nv[0], j, nj - 1)

    grid_spec = pltpu.PrefetchScalarGridSpec(
        num_scalar_prefetch=2,
        grid=(xs.shape[0] // T_MOE, nj),
        in_specs=[
            pl.BlockSpec((T_MOE, D), lambda i, j, te, nv: (i, 0)),
            pl.BlockSpec((1, 1, D, T_FF), lambda i, j, te, nv: (layer, te[i], 0, jj(i, j, nv))),
            pl.BlockSpec((1, 1, D, T_FF), lambda i, j, te, nv: (layer, te[i], 0, jj(i, j, nv))),
            pl.BlockSpec((1, 1, T_FF, D), lambda i, j, te, nv: (layer, te[i], jj(i, j, nv), 0)),
        ],
        out_specs=pl.BlockSpec((T_MOE, D), lambda i, j, te, nv: (i, 0)),
        scratch_shapes=[pltpu.VMEM((T_MOE, D), BF16), pltpu.VMEM((T_MOE, D), F32)],
    )
    return pl.pallas_call(
        _ffn_moe_body,
        grid_spec=grid_spec,
        out_shape=jax.ShapeDtypeStruct(xs.shape, F32),
        compiler_params=pltpu.CompilerParams(
            dimension_semantics=("arbitrary", "arbitrary"), vmem_limit_bytes=VMEM_BIG),
        name="ffn_moe",
    )(tile_expert, n_valid, xs, w1, w3, w2)


def _combine_body(start_ref, len_ref, s1_ref, s2_ref, w1_ref, w2_ref, x_ref, gate_ref, os_ref, o_ref,
                  lo_ref, sem):
    i = pl.program_id(0)
    last = pl.num_programs(0) - 1
    slot = i % 2
    other = 1 - slot
    os_at = lambda a, p: os_ref.at[pl.ds(a, p)]

    def gather(tile, buf, start, enable=True):
        plan = [(ln, far, near) for ln, near, far in _tile_plan(tile, len_ref, start_ref)]
        _segment_copies(plan, os_at, lambda a, p: lo_ref.at[buf, pl.ds(a, p)], sem.at[buf], start, enable)

    @pl.when(i == 0)
    def _():
        lo_ref[...] = jnp.zeros_like(lo_ref)
        gather(i, slot, start=True)

    gather(jnp.minimum(i + 1, last), other, start=True, enable=i < last)
    gather(i, slot, start=False)

    lo = lo_ref[slot].astype(BF16)
    lane = lax.broadcasted_iota(I32, (T_ROUTE, S_LOC), 1)
    pick1 = jnp.where(lane == s1_ref[...], 1.0, 0.0).astype(BF16)
    pick2 = jnp.where(lane == s2_ref[...], 1.0, 0.0).astype(BF16)
    y = (w1_ref[...] * jnp.dot(pick1, lo, preferred_element_type=F32)
         + w2_ref[...] * jnp.dot(pick2, lo, preferred_element_type=F32))
    o_ref[...] = x_ref[...] + gate_ref[0] * y


def _combine(x, n, gate, seg_start, seg_len, s1, s2, w1, w2, o_sorted):
    nt = n // T_ROUTE
    col = lambda a: a.reshape(n, 1)
    col_spec = pl.BlockSpec((T_ROUTE, 1), lambda i, *_: (i, 0))
    grid_spec = pltpu.PrefetchScalarGridSpec(
        num_scalar_prefetch=2,
        grid=(nt,),
        in_specs=[col_spec, col_spec, col_spec, col_spec,
                  pl.BlockSpec((T_ROUTE, D), lambda i, *_: (i, 0)),
                  _mod_spec(T_ROUTE),
                  pl.BlockSpec(memory_space=pl.ANY)],
        out_specs=pl.BlockSpec((T_ROUTE, D), lambda i, *_: (i, 0)),
        scratch_shapes=[pltpu.VMEM((2, S_LOC, D), F32), pltpu.SemaphoreType.DMA((2,))],
    )
    return pl.pallas_call(
        _combine_body,
        grid_spec=grid_spec,
        out_shape=jax.ShapeDtypeStruct((n, D), F32),
        compiler_params=pltpu.CompilerParams(
            dimension_semantics=("arbitrary",), vmem_limit_bytes=VMEM_BIG),
        name="moe_combine",
    )(seg_start.reshape(-1), seg_len.reshape(-1), col(s1), col(s2), col(w1), col(w2), x, gate, o_sorted)


def _moe(x, n, sh, sc, gate, ng, router, w1, w3, w2, layer):
    nt = n // T_ROUTE
    s1, s2, p1, p2, pc = _router(x, n, sh, sc, ng, router)
    seg_len = pc[:, :, 0]
    total = jnp.sum(seg_len, axis=0)
    tiles_per = (total + T_MOE - 1) // T_MOE
    tile_ends = jnp.cumsum(tiles_per)
    group_start = (tile_ends - tiles_per) * T_MOE
    seg_start = group_start[None, :] + jnp.cumsum(seg_len, axis=0) - seg_len
    tail_start = group_start + total
    tail_len = tiles_per * T_MOE - total
    n_valid = tile_ends[-1:]
    n_tiles = -(-(2 * n + nt * N_EXPERTS * SEG_ALIGN) // T_MOE) + N_EXPERTS
    tile_ids = jnp.minimum(jnp.arange(n_tiles, dtype=I32), n_valid - 1)
    tile_expert = jnp.sum((tile_ids[:, None] >= tile_ends[None, :]).astype(I32), axis=1)
    xs = _dispatch(x, n, sh, sc, ng, s1, s2, seg_start, seg_len, tail_start, tail_len, n_valid, n_tiles)
    o_sorted = _ffn_moe(xs, tile_expert, n_valid, w1, w3, w2, layer)
    return _combine(x, n, gate, seg_start, seg_len, s1, s2, p1, p2, o_sorted)


def _swap_halves(y):
    q = HEAD_DIM // 4
    return jnp.concatenate([y[q:2 * q], y[:q], y[3 * q:], y[2 * q:3 * q]], axis=0)


def _norm_rope_t(zh, c_tab, s_tab):
    ss = jnp.mean(zh * zh, axis=0, keepdims=True)
    yn = zh * lax.rsqrt(ss + EPS)
    return yn * c_tab + _swap_halves(yn) * s_tab


def _qkv_body(x_ref, sh_ref, sc_ref, ng_ref, w_ref, cq_ref, sq_ref, ck_ref, sk_ref, q_ref, k_ref, v_ref):
    h = _modnorm(x_ref[...], ng_ref[...], sc_ref[0], sh_ref[0]).astype(BF16)
    zt = lax.dot_general(w_ref[...], h, (((1,), (1,)), ((), ())), preferred_element_type=F32)
    nq = N_HEADS * HEAD_DIM
    nkv = N_KV * HEAD_DIM
    cq, sq, ck, sk = cq_ref[...], sq_ref[...], ck_ref[...], sk_ref[...]
    for hd in range(N_HEADS):
        lo = hd * HEAD_DIM
        q_ref[lo:lo + HEAD_DIM, :] = _norm_rope_t(zt[lo:lo + HEAD_DIM], cq, sq).astype(BF16)
    kt = jnp.concatenate(
        [_norm_rope_t(zt[nq + kh * HEAD_DIM:nq + (kh + 1) * HEAD_DIM], ck, sk) for kh in range(N_KV)],
        axis=0)
    k_ref[...] = kt.T.astype(BF16)
    v_ref[...] = zt[nq + nkv:].astype(BF16)


def _rope_tables(q_g, k_g):
    half = HEAD_DIM // 2
    quarter = HEAD_DIM // 4
    pos = jnp.arange(SEQ)
    pos_row = (pos // GRID_W).astype(F32)
    pos_col = (pos % GRID_W).astype(F32)
    inv_freq = ROPE_THETA ** (-jnp.arange(0, half, 2, dtype=F32) / half)
    ang_row = inv_freq[:, None] * pos_row[None, :]
    ang_col = inv_freq[:, None] * pos_col[None, :]
    ang = jnp.concatenate([ang_row, ang_row, ang_col, ang_col], axis=0)
    cos = jnp.concatenate([jnp.cos(ang), jnp.ones((HEAD_DIM, T_QKV), F32)], axis=1)
    sin = jnp.concatenate([jnp.sin(ang), jnp.zeros((HEAD_DIM, T_QKV), F32)], axis=1)
    first = ((jnp.arange(HEAD_DIM) % half) < quarter)[:, None]
    sin = jnp.where(first, -sin, sin)

    def tables(g, scale):
        g = g.astype(F32) * scale
        partner = _swap_halves(g[:, None])
        return g[:, None] * cos, partner * sin

    return tables(q_g, HEAD_DIM ** -0.5 * LOG2_E) + tables(k_g, 1.0)


def _qkv(x, sh, sc, ng, w_qkv, q_g, k_g):
    n = x.shape[0]
    nt = n // T_QKV
    nkv = N_KV * HEAD_DIM
    wd = w_qkv.shape[1]
    per_seq = SEQ // T_QKV
    tab_spec = pl.BlockSpec(
        (HEAD_DIM, T_QKV), lambda i: (0, jnp.where(i < N_LAT // T_QKV, i % per_seq, per_seq)))
    mod = _mod_spec(T_QKV)
    return pl.pallas_call(
        _qkv_body,
        grid=(nt,),
        in_specs=[
            pl.BlockSpec((T_QKV, D), lambda i: (i, 0)),
            mod, mod,
            _const_spec((1, D)),
            _const_spec((wd, D)),
            tab_spec, tab_spec, tab_spec, tab_spec,
        ],
        out_specs=[pl.BlockSpec((D, T_QKV), lambda i: (0, i)),
                   pl.BlockSpec((T_QKV, nkv), lambda i: (i, 0)),
                   pl.BlockSpec((nkv, T_QKV), lambda i: (0, i))],
        out_shape=[jax.ShapeDtypeStruct((D, n), BF16),
                   jax.ShapeDtypeStruct((n, nkv), BF16),
                   jax.ShapeDtypeStruct((nkv, n), BF16)],
        compiler_params=pltpu.CompilerParams(
            dimension_semantics=("arbitrary",), vmem_limit_bytes=VMEM_BIG),
        name="attn_qkv",
    )(x, sh, sc, ng.reshape(1, D), w_qkv.T.astype(BF16), *_rope_tables(q_g, k_g))


def _attn_heads(qt_ref, k_all, vt_all, band, sink_ref, o_ref):
    group = N_HEADS // N_KV
    nk = k_all.shape[0]
    ones = jnp.ones((BF16_ROWS, nk), BF16)
    zeros = jnp.zeros((HEAD_DIM, group * T_Q), BF16)
    outs = []
    scores = []
    probs = []
    for kh in range(N_KV):
        q4 = jnp.concatenate(
            [qt_ref[(kh * group + g) * HEAD_DIM:(kh * group + g + 1) * HEAD_DIM, :] for g in range(group)],
            axis=1)
        qpad = jnp.concatenate([q4, zeros] if kh % 2 == 0 else [zeros, q4], axis=0)
        k2 = k_all[:, (kh // 2) * LANES:(kh // 2 + 1) * LANES]
        scores.append(jnp.dot(k2, qpad, preferred_element_type=F32))
    for kh in range(N_KV):
        st = scores[kh]
        blocks = [st[c * T_Q:(c + 1) * T_Q] for c in range(nk // T_Q)]
        if band is not None:
            blocks[0] = jnp.where(band[0], blocks[0], NEG_INF)
            blocks[2] = jnp.where(band[1], blocks[2], NEG_INF)
        best = blocks[0]
        for blk in blocks[1:]:
            best = jnp.maximum(best, blk)
        sink = jnp.concatenate(
            [jnp.full((1, T_Q), sink_ref[kh * group + g] * LOG2_E, F32) for g in range(group)], axis=1)
        m = jnp.maximum(jnp.max(best, axis=0, keepdims=True), sink)
        pt = jnp.concatenate([jnp.exp2(blk - m).astype(BF16) for blk in blocks], axis=0)
        probs.append((pt, jnp.exp2(sink - m)))
    for kh in range(N_KV):
        pt, sink_term = probs[kh]
        vt_aug = jnp.concatenate([vt_all[kh * HEAD_DIM:(kh + 1) * HEAD_DIM, :], ones], axis=0)
        ot = jnp.dot(vt_aug, pt, preferred_element_type=F32)
        den = ot[HEAD_DIM:HEAD_DIM + 1] + sink_term
        o = ot[:HEAD_DIM] / den
        outs.extend(o[:, g * T_Q:(g + 1) * T_Q] for g in range(group))
    o_ref[...] = jnp.concatenate(outs, axis=0).T.astype(BF16)


def _attn_body(sink_ref, q_ref, kp_ref, kc_ref, kn_ref, kx_ref, vp_ref, vc_ref, vn_ref, vx_ref, o_ref,
               *, n_q_blocks):
    iq = pl.program_id(1)
    per_seq = SEQ // T_Q
    group = N_HEADS // N_KV

    @pl.when(iq < per_seq)
    def _():
        k_all = jnp.concatenate([kp_ref[...], kc_ref[...], kn_ref[...], kx_ref[...]], axis=0)
        vt_all = jnp.concatenate([vp_ref[...], vc_ref[...], vn_ref[...], vx_ref[...]], axis=1)
        c = lax.broadcasted_iota(I32, (T_Q, group * T_Q), 0)
        r = lax.broadcasted_iota(I32, (T_Q, group * T_Q), 1) % T_Q
        far = 2 * T_Q
        mask_prev = c >= r + jnp.where(iq > 0, 0, far)
        mask_next = c <= r - jnp.where(iq < per_seq - 1, 0, far)
        _attn_heads(q_ref, k_all, vt_all, (mask_prev, mask_next), sink_ref, o_ref)

    if n_q_blocks > per_seq:
        @pl.when(iq >= per_seq)
        def _():
            _attn_heads(q_ref, kx_ref[...], vx_ref[...], None, sink_ref, o_ref)


def _attention(qt, k, vt, sink, need_ctx):
    assert T_Q == WINDOW
    per_seq = SEQ // T_Q
    ctx_blocks = CTX_LEN // T_Q
    n_q_blocks = per_seq + (ctx_blocks if need_ctx else 0)
    n_out = N_ALL if need_ctx else N_LAT
    lat_blocks = N_LAT // T_Q
    kw = k.shape[1]

    def q_blk(b, iq):
        return jnp.where(iq < per_seq, b * per_seq + iq, lat_blocks + b * ctx_blocks + (iq - per_seq))

    def win_blk(b, iq, off):
        return b * per_seq + jnp.clip(iq + off, 0, per_seq - 1)

    ctx_blk = lambda b: N_LAT // CTX_LEN + b
    k_win = lambda off: pl.BlockSpec((T_Q, kw), lambda b, iq, *_: (win_blk(b, iq, off), 0))
    v_win = lambda off: pl.BlockSpec((kw, T_Q), lambda b, iq, *_: (0, win_blk(b, iq, off)))
    k_ctx = pl.BlockSpec((CTX_LEN, kw), lambda b, iq, *_: (ctx_blk(b), 0))
    v_ctx = pl.BlockSpec((kw, CTX_LEN), lambda b, iq, *_: (0, ctx_blk(b)))
    grid_spec = pltpu.PrefetchScalarGridSpec(
        num_scalar_prefetch=1,
        grid=(NB, n_q_blocks),
        in_specs=[pl.BlockSpec((D, T_Q), lambda b, iq, *_: (0, q_blk(b, iq))),
                  k_win(-1), k_win(0), k_win(1), k_ctx, v_win(-1), v_win(0), v_win(1), v_ctx],
        out_specs=pl.BlockSpec((T_Q, D), lambda b, iq, *_: (q_blk(b, iq), 0)),
    )
    return pl.pallas_call(
        functools.partial(_attn_body, n_q_blocks=n_q_blocks),
        grid_spec=grid_spec,
        out_shape=jax.ShapeDtypeStruct((n_out, D), BF16),
        compiler_params=pltpu.CompilerParams(
            dimension_semantics=("arbitrary", "arbitrary"), vmem_limit_bytes=VMEM_BIG),
        name="attn_core",
    )(sink.astype(F32), qt, k, k, k, k, vt, vt, vt, vt)


def _proj_body(a_ref, x_ref, gate_ref, w_ref, o_ref):
    y = jnp.dot(a_ref[...], w_ref[...], preferred_element_type=F32)
    o_ref[...] = x_ref[...] + gate_ref[0] * y


def _proj_residual(a, x, gate, w):
    n = a.shape[0]
    return pl.pallas_call(
        _proj_body,
        grid=(n // T_PROJ,),
        in_specs=[pl.BlockSpec((T_PROJ, D), lambda i: (i, 0)),
                  pl.BlockSpec((T_PROJ, D), lambda i: (i, 0)),
                  _mod_spec(T_PROJ),
                  _const_spec((D, D))],
        out_specs=pl.BlockSpec((T_PROJ, D), lambda i: (i, 0)),
        out_shape=jax.ShapeDtypeStruct((n, D), F32),
        compiler_params=pltpu.CompilerParams(dimension_semantics=("arbitrary",)),
        name="attn_proj",
    )(a, x, gate, w.astype(BF16))


def kernel(x, c, ctx, c_ctx, ada_w, ada_b, norm_mix_g, norm_ffn_g, ev_w_in, ev_ln_g, ev_ln_b, ev_ws,
           ev_bs, ev_conv_w, ev_conv_b, ev_cnorm_g, ev_w_out, od_w_qkv, od_q_g, od_k_g, od_sink, od_w_o,
           ff_w1, ff_w3, ff_w2, moe_router, moe_w1, moe_w3, moe_w2):
    assert x.shape == (NB, SEQ, D) and ctx.shape == (NB, CTX_LEN, D)
    mods = _ada_mods(c, c_ctx, ada_w, ada_b)
    xa = jnp.concatenate([x.reshape(N_LAT, D), ctx.reshape(N_CTX, D)], axis=0)
    ev_w_in_b, ev_w_out_b = ev_w_in.astype(BF16), ev_w_out.astype(BF16)
    for li in range(DEPTH):
        need_ctx = li < DEPTH - 1
        j = li // 2
        sh1, sc1, g1, sh2, sc2, g2 = mods[li]
        if li % 2 == 0:
            xa = _even_mixer(xa, sh1, sc1, g1, norm_mix_g[li], ev_w_in_b, ev_ln_g[j], ev_ln_b[j],
                             ev_ws[j], ev_bs[j], ev_conv_w[j], ev_conv_b[j], ev_cnorm_g[j], ev_w_out_b, j)
            xa = _ffn_dense(xa, sh2, sc2, g2, norm_ffn_g[li], ff_w1, ff_w3, ff_w2, j)
        else:
            qt, k, vt = _qkv(xa, sh1, sc1, norm_mix_g[li], od_w_qkv[j], od_q_g[j], od_k_g[j])
            o = _attention(qt, k, vt, od_sink[j], need_ctx)
            xa = _proj_residual(o, xa, g1, od_w_o[j])
            n = N_ALL if need_ctx else N_LAT
            xa = _moe(xa, n, sh2, sc2, g2, norm_ffn_g[li], moe_router[j], moe_w1, moe_w3, moe_w2, j)
    return xa[:N_LAT].reshape(NB, SEQ, D)
```

```python
import functools

import jax
import jax.numpy as jnp
from jax import lax
from jax.experimental import pallas as pl
from jax.experimental.pallas import tpu as pltpu

F32 = jnp.float32
BF16 = jnp.bfloat16
I32 = jnp.int32

D = 1024
NB = 8
SEQ = 2048
CTX_LEN = 256
DEPTH = 4
GRID_W = 64
CHUNK = 128
A_GROUPS = 8
CONV_W = 31
N_HEADS = 16
N_KV = 4
HEAD_DIM = 64
WINDOW = 128
ROPE_THETA = 10000.0
D_FF = 3584
N_EXPERTS = 8
EPS = 1e-6
NEG_INF = -1e30
LOG2_E = 1.4426950408889634

N_LAT = NB * SEQ
N_CTX = NB * CTX_LEN
N_ALL = N_LAT + N_CTX
CTX_MOD_ROW = NB

LANES = 128
SUBLANES = 8
BF16_ROWS = 16

T_EVEN = 256
EVEN_PAIR = 2
HALO = BF16_ROWS
T_FFN = 1024
T_FF = 512
T_ROUTE = 512
T_MOE = 1024
MOE_PARTS = 4
SEG_ALIGN = SUBLANES
S_LOC = 2 * T_ROUTE + N_EXPERTS * SEG_ALIGN
SEG_SIZES = tuple(T_ROUTE >> s for s in range(T_ROUTE.bit_length()) if T_ROUTE >> s >= SEG_ALIGN)
assert SEG_SIZES[-1] == SEG_ALIGN
T_QKV = 256
T_Q = 128
T_PROJ = 512

VMEM_BIG = 52 * 1024 * 1024


def _mod_spec(tile):
    per = SEQ // tile
    return pl.BlockSpec((1, 1, D), lambda i, *_: (jnp.minimum(i // per, CTX_MOD_ROW), 0, 0))


def _const_spec(shape):
    nd = len(shape)
    return pl.BlockSpec(shape, lambda *_: (0,) * nd)


def _modnorm(x, g, sc, sh):
    ms = jnp.mean(x * x, axis=-1, keepdims=True)
    return (x * lax.rsqrt(ms + EPS) * g) * (1.0 + sc) + sh


def _silu(x):
    return x * jax.nn.sigmoid(x)


def _ada_body(c_ref, w_ref, b_ref, o_ref):
    a = _silu(c_ref[...])
    o_ref[0] = jnp.dot(a.astype(BF16), w_ref[0].astype(BF16), preferred_element_type=F32) + b_ref[0]


def _ada_mods(c, c_ctx, ada_w, ada_b):
    rows = 16
    cc = jnp.concatenate([c, c_ctx[None, :], jnp.zeros((rows - NB - 1, D), F32)], axis=0)
    out = pl.pallas_call(
        _ada_body,
        grid=(DEPTH, 6),
        in_specs=[
            pl.BlockSpec((rows, D), lambda l, n: (0, 0)),
            pl.BlockSpec((1, D, D), lambda l, n: (l, 0, n)),
            pl.BlockSpec((1, 1, D), lambda l, n: (l, 0, n)),
        ],
        out_specs=pl.BlockSpec((1, rows, D), lambda l, n: (l, 0, n)),
        out_shape=jax.ShapeDtypeStruct((DEPTH, rows, 6 * D), F32),
        name="ada_mod",
    )(cc, ada_w, ada_b.reshape(DEPTH, 1, 6 * D))
    m = out[:, :NB + 1].reshape(DEPTH, NB + 1, 6, 1, D)
    return [[m[l, :, k] for k in range(6)] for l in range(DEPTH)]


def _even_body(x_ref, xp_ref, xn_ref, sh_ref, sc_ref, gate_ref, ng_ref, win_ref, lng_ref, lnb_ref,
               ws_ref, bs_ref, cw_ref, cb_ref, cng_ref, wout_ref, o_ref, gext_ref):
    step = pl.program_id(0)
    per_seq = SEQ // T_EVEN
    ext = T_EVEN + 2 * HALO
    halves = range(EVEN_PAIR)
    tiles = [x_ref[hf * T_EVEN:(hf + 1) * T_EVEN] for hf in halves]
    befores = [xp_ref[...]] + [tiles[hf - 1][T_EVEN - HALO:] for hf in halves[1:]]
    afters = [tiles[hf + 1][:HALO] for hf in halves[:-1]] + [xn_ref[...]]

    hs = [_modnorm(jnp.concatenate([befores[hf], tiles[hf], afters[hf]], axis=0),
                   ng_ref[...], sc_ref[0], sh_ref[0]).astype(BF16) for hf in halves]
    zbs = [jnp.dot(hs[hf], win_ref[0, :, 2 * D:], preferred_element_type=F32) for hf in halves]
    zas = [jnp.dot(hs[hf][HALO:HALO + T_EVEN], win_ref[0, :, :2 * D], preferred_element_type=F32)
           for hf in halves]

    for hf in halves:
        i = step * EVEN_PAIR + hf
        is_lat = i < N_LAT // T_EVEN
        pos = i % per_seq
        is_start = jnp.logical_or(jnp.logical_not(is_lat), pos == 0)
        is_end = jnp.logical_or(jnp.logical_not(is_lat), pos == per_seq - 1)
        gg = zbs[hf][:, :D] * jax.nn.sigmoid(zbs[hf][:, D:])
        row = lax.broadcasted_iota(I32, (ext, 1), 0)
        lo = jnp.where(is_start, HALO, 0)
        hi = jnp.where(is_end, T_EVEN + HALO, ext)
        gg = jnp.where(jnp.logical_and(row >= lo, row < hi), gg, 0.0)
        for cbk in range(D // LANES):
            gext_ref[hf, cbk] = gg[:, cbk * LANES:(cbk + 1) * LANES]

    rows_per = 64
    gdim = D // A_GROUPS
    for hf in halves:
        cols = []
        for cbk in range(D // LANES):
            blocks = []
            for rb in range(T_EVEN // rows_per):
                acc = jnp.zeros((rows_per, LANES), F32)
                for k in range(CONV_W):
                    off = rb * rows_per + k + HALO - CONV_W // 2
                    acc = acc + cw_ref[cbk, pl.ds(k, 1), :] * gext_ref[hf, cbk, pl.ds(off, rows_per), :]
                blocks.append(acc)
            cols.append(jnp.concatenate(blocks, axis=0))

        cv = jnp.concatenate(cols, axis=1) + cb_ref[...]
        ms = jnp.mean(cv * cv, axis=-1, keepdims=True)
        yb = _silu(cv * lax.rsqrt(ms + EPS) * cng_ref[...])

        u = jax.nn.gelu(zas[hf][:, :D])
        v = jax.nn.gelu(zas[hf][:, D:])
        mu = jnp.mean(v, axis=-1, keepdims=True)
        vc = v - mu
        var = jnp.mean(vc * vc, axis=-1, keepdims=True)
        vn = (vc * lax.rsqrt(var + EPS) * lng_ref[...] + lnb_ref[...]).astype(BF16)
        chunks = []
        for ck in range(T_EVEN // CHUNK):
            blocks = [
                jnp.dot(ws_ref[g], vn[ck * CHUNK:(ck + 1) * CHUNK, g * gdim:(g + 1) * gdim],
                        preferred_element_type=F32)
                for g in range(A_GROUPS)
            ]
            chunks.append(jnp.concatenate(blocks, axis=1) + bs_ref[...])
        ya = u * jnp.concatenate(chunks, axis=0)

        y = (jnp.dot(ya.astype(BF16), wout_ref[0, :D], preferred_element_type=F32)
             + jnp.dot(yb.astype(BF16), wout_ref[0, D:], preferred_element_type=F32))
        o_ref[hf * T_EVEN:(hf + 1) * T_EVEN, :] = tiles[hf] + gate_ref[0] * y


def _even_mixer(x, sh, sc, gate, ng, w_in, ln_g, ln_b, ws, bs, conv_w, conv_b, cn_g, w_out, j):
    n = x.shape[0]
    rows = EVEN_PAIR * T_EVEN
    nt = n // rows
    hb = rows // HALO
    last = n // HALO - 1
    ncb = D // LANES
    bs_full = jnp.repeat(bs.T, D // A_GROUPS, axis=1)
    cw = jnp.pad(conv_w, ((0, 32 - CONV_W), (0, 0))).reshape(32, ncb, LANES).transpose(1, 0, 2)
    mod = _mod_spec(rows)
    row1 = lambda a: a.reshape(1, D)
    return pl.pallas_call(
        _even_body,
        grid=(nt,),
        in_specs=[
            pl.BlockSpec((rows, D), lambda i: (i, 0)),
            pl.BlockSpec((HALO, D), lambda i: (jnp.maximum(i * hb - 1, 0), 0)),
            pl.BlockSpec((HALO, D), lambda i: (jnp.minimum((i + 1) * hb, last), 0)),
            mod, mod, mod,
            _const_spec((1, D)),
            pl.BlockSpec((1, D, 4 * D), lambda i: (j, 0, 0)),
            _const_spec((1, D)), _const_spec((1, D)),
            _const_spec((A_GROUPS, CHUNK, CHUNK)),
            _const_spec((CHUNK, D)),
            _const_spec((ncb, 32, LANES)),
            _const_spec((1, D)), _const_spec((1, D)),
            pl.BlockSpec((1, 2 * D, D), lambda i: (j, 0, 0)),
        ],
        out_specs=pl.BlockSpec((rows, D), lambda i: (i, 0)),
        out_shape=jax.ShapeDtypeStruct((n, D), F32),
        scratch_shapes=[
            pltpu.VMEM((EVEN_PAIR, ncb, T_EVEN + 2 * HALO, LANES), F32),
        ],
        compiler_params=pltpu.CompilerParams(
            dimension_semantics=("arbitrary",), vmem_limit_bytes=VMEM_BIG),
        name="even_mixer",
    )(x, x, x, sh, sc, gate, row1(ng), w_in, row1(ln_g), row1(ln_b), ws.astype(BF16),
      bs_full, cw, row1(conv_b), row1(cn_g), w_out)


def _ffn_dense_body(x_ref, sh_ref, sc_ref, gate_ref, ng_ref, w1_ref, w3_ref, w2_ref, o_ref,
                    h_ref, acc_ref):
    j = pl.program_id(1)

    @pl.when(j == 0)
    def _():
        h_ref[...] = _modnorm(x_ref[...], ng_ref[...], sc_ref[0], sh_ref[0]).astype(BF16)
        acc_ref[...] = jnp.zeros_like(acc_ref)

    h = h_ref[...]
    a = jnp.dot(h, w1_ref[0].astype(BF16), preferred_element_type=F32)
    b = jnp.dot(h, w3_ref[0].astype(BF16), preferred_element_type=F32)
    t = (_silu(a) * b).astype(BF16)
    acc_ref[...] += jnp.dot(t, w2_ref[0].astype(BF16), preferred_element_type=F32)

    @pl.when(j == pl.num_programs(1) - 1)
    def _():
        o_ref[...] = x_ref[...] + gate_ref[0] * acc_ref[...]


def _ffn_dense(x, sh, sc, gate, ng, w1, w3, w2, layer):
    n = x.shape[0]
    mod = _mod_spec(T_FFN)
    return pl.pallas_call(
        _ffn_dense_body,
        grid=(n // T_FFN, D_FF // T_FF),
        in_specs=[
            pl.BlockSpec((T_FFN, D), lambda i, j: (i, 0)),
            mod, mod, mod,
            _const_spec((1, D)),
            pl.BlockSpec((1, D, T_FF), lambda i, j: (layer, 0, j)),
            pl.BlockSpec((1, D, T_FF), lambda i, j: (layer, 0, j)),
            pl.BlockSpec((1, T_FF, D), lambda i, j: (layer, j, 0)),
        ],
        out_specs=pl.BlockSpec((T_FFN, D), lambda i, j: (i, 0)),
        out_shape=jax.ShapeDtypeStruct((n, D), F32),
        scratch_shapes=[pltpu.VMEM((T_FFN, D), BF16), pltpu.VMEM((T_FFN, D), F32)],
        compiler_params=pltpu.CompilerParams(
            dimension_semantics=("arbitrary", "arbitrary"), vmem_limit_bytes=VMEM_BIG),
        name="ffn_dense",
    )(x, sh, sc, gate, ng.reshape(1, D), w1, w3, w2)


def _router_body(x_ref, sh_ref, sc_ref, ng_ref, rt_ref, s1_ref, s2_ref, w1_ref, w2_ref, pc_ref):
    h = _modnorm(x_ref[...], ng_ref[...], sc_ref[0], sh_ref[0])
    lg = lax.dot_general(rt_ref[...], h, (((1,), (1,)), ((), ())),
                         precision=lax.Precision.HIGHEST, preferred_element_type=F32)
    e = lax.broadcasted_iota(I32, lg.shape, 0).astype(F32)
    big = float(N_EXPERTS)
    m1 = jnp.max(lg, axis=0, keepdims=True)
    i1 = jnp.min(jnp.where(lg == m1, e, big), axis=0, keepdims=True)
    lg2 = jnp.where(e == i1, -jnp.inf, lg)
    m2 = jnp.max(lg2, axis=0, keepdims=True)
    i2 = jnp.min(jnp.where(lg2 == m2, e, big), axis=0, keepdims=True)
    e2 = jnp.exp(m2 - m1)
    den = 1.0 + e2
    w1_ref[0] = 1.0 / den
    w2_ref[0] = e2 / den

    sel1 = e == i1
    sel2 = e == i2
    member = jnp.where(jnp.logical_or(sel1, sel2), 1.0, 0.0)
    t = lg.shape[1]
    before = (lax.broadcasted_iota(I32, (t, t), 0) < lax.broadcasted_iota(I32, (t, t), 1))
    tri = jnp.where(before, 1.0, 0.0).astype(BF16)
    rank = jnp.dot(member.astype(BF16), tri, preferred_element_type=F32)
    count = jnp.sum(member, axis=1, keepdims=True)
    padded = jnp.ceil(count * (1.0 / SEG_ALIGN)) * SEG_ALIGN
    e_col = lax.broadcasted_iota(I32, (N_EXPERTS, 1), 0)
    seg_start = jnp.zeros((N_EXPERTS, 1), F32)
    for ex in range(1, N_EXPERTS):
        below = jnp.sum(jnp.where(e_col < ex, padded, 0.0), axis=0, keepdims=True)
        seg_start = jnp.where(e_col == ex, below, seg_start)
    slot = rank + seg_start
    s1 = jnp.sum(jnp.where(sel1, slot, 0.0), axis=0, keepdims=True).astype(I32)
    s2 = jnp.sum(jnp.where(sel2, slot, 0.0), axis=0, keepdims=True).astype(I32)
    s1_ref[0] = s1
    s2_ref[0] = s2
    pc_ref[0] = jnp.broadcast_to(padded.astype(I32), (N_EXPERTS, LANES))


def _router(x, n, sh, sc, ng, router):
    nt = n // T_ROUTE
    mod = _mod_spec(T_ROUTE)
    vec_spec = pl.BlockSpec((1, 1, T_ROUTE), lambda i: (i, 0, 0))
    vec_i = jax.ShapeDtypeStruct((nt, 1, T_ROUTE), I32)
    vec_f = jax.ShapeDtypeStruct((nt, 1, T_ROUTE), F32)
    return pl.pallas_call(
        _router_body,
        grid=(nt,),
        in_specs=[
            pl.BlockSpec((T_ROUTE, D), lambda i: (i, 0)),
            mod, mod,
            _const_spec((1, D)),
            _const_spec((N_EXPERTS, D)),
        ],
        out_specs=[vec_spec, vec_spec, vec_spec, vec_spec,
                   pl.BlockSpec((1, N_EXPERTS, LANES), lambda i: (i, 0, 0))],
        out_shape=[vec_i, vec_i, vec_f, vec_f, jax.ShapeDtypeStruct((nt, N_EXPERTS, LANES), I32)],
        compiler_params=pltpu.CompilerParams(dimension_semantics=("arbitrary",)),
        name="moe_router",
    )(x, sh, sc, ng.reshape(1, D), router.T)


def _dispatch_body(start_ref, len_ref, tail_start_ref, tail_len_ref, nv_ref,
                   x_ref, sh_ref, sc_ref, ng_ref, s1_ref, s2_ref, xs_ref, lbuf_ref, zbuf_ref, sem,
                   *, n_tiles, min_tiles):
    i = pl.program_id(0)
    slot = i % 2
    other = 1 - slot
    h = _modnorm(x_ref[...], ng_ref[...], sc_ref[0], sh_ref[0]).astype(BF16)
    rows = lax.broadcasted_iota(I32, (S_LOC, T_ROUTE), 0)
    perm = jnp.where(jnp.logical_or(rows == s1_ref[0], rows == s2_ref[0]), 1.0, 0.0).astype(BF16)
    lbuf_ref[slot] = jnp.dot(perm, h, preferred_element_type=F32)

    xs_at = lambda a, p: xs_ref.at[pl.ds(a, p)]
    _segment_copies(_tile_plan(i, len_ref, start_ref), lambda a, p: lbuf_ref.at[slot, pl.ds(a, p)], xs_at,
                    sem.at[slot], start=True)
    _segment_copies(_tile_plan(jnp.maximum(i - 1, 0), len_ref, start_ref),
                    lambda a, p: lbuf_ref.at[other, pl.ds(a, p)], xs_at, sem.at[other], start=False,
                    enable=i > 0)

    @pl.when(i == pl.num_programs(0) - 1)
    def _():
        _segment_copies(_tile_plan(i, len_ref, start_ref), lambda a, p: lbuf_ref.at[slot, pl.ds(a, p)],
                        xs_at, sem.at[slot], start=False)
        zbuf_ref[...] = jnp.zeros_like(zbuf_ref)
        tails = [(tail_len_ref[ex], 0, tail_start_ref[ex]) for ex in range(N_EXPERTS)]
        zero_at = lambda a, p: zbuf_ref.at[pl.ds(a, p)]
        _segment_copies(tails, zero_at, xs_at, sem.at[slot], start=True)
        _segment_copies(tails, zero_at, xs_at, sem.at[slot], start=False)
        for tile in range(min_tiles, n_tiles):
            @pl.when(tile >= nv_ref[0])
            def _(tile=tile):
                cp = pltpu.make_async_copy(zbuf_ref, xs_ref.at[pl.ds(tile * T_MOE, T_MOE)], sem.at[slot])
                cp.start()
                cp.wait()


def _dispatch(x, n, sh, sc, ng, s1, s2, seg_start, seg_len, tail_start, tail_len, n_valid, n_tiles):
    nt = n // T_ROUTE
    mod = _mod_spec(T_ROUTE)
    vec_spec = pl.BlockSpec((1, 1, T_ROUTE), lambda i, *_: (i, 0, 0))
    grid_spec = pltpu.PrefetchScalarGridSpec(
        num_scalar_prefetch=5,
        grid=(nt,),
        in_specs=[pl.BlockSpec((T_ROUTE, D), lambda i, *_: (i, 0)),
                  mod, mod,
                  pl.BlockSpec((1, D), lambda i, *_: (0, 0)),
                  vec_spec, vec_spec],
        out_specs=pl.BlockSpec(memory_space=pl.ANY),
        scratch_shapes=[pltpu.VMEM((2, S_LOC, D), F32), pltpu.VMEM((T_MOE, D), F32),
                        pltpu.SemaphoreType.DMA((2,))],
    )
    return pl.pallas_call(
        functools.partial(_dispatch_body, n_tiles=n_tiles, min_tiles=(2 * n) // T_MOE),
        grid_spec=grid_spec,
        out_shape=jax.ShapeDtypeStruct((n_tiles * T_MOE, D), F32),
        compiler_params=pltpu.CompilerParams(
            dimension_semantics=("arbitrary",), vmem_limit_bytes=VMEM_BIG),
        name="moe_dispatch",
    )(seg_start.reshape(-1), seg_len.reshape(-1), tail_start, tail_len, n_valid,
      x, sh, sc, ng.reshape(1, D), s1, s2)


def _tile_plan(tile, len_ref, far_ref):
    near = 0
    plan = []
    for ex in range(N_EXPERTS):
        ln = len_ref[tile * N_EXPERTS + ex]
        plan.append((ln, near, far_ref[tile * N_EXPERTS + ex]))
        near = near + ln
    return plan


def _segment_copies(plan, src_at, dst_at, sem, start, enable=True):
    for ln, src, dst in plan:
        for p in SEG_SIZES:
            off = jnp.bitwise_and(ln, -(2 * p))

            @pl.when(jnp.logical_and(jnp.bitwise_and(ln, p) != 0, enable))
            def _(off=off, src=src, dst=dst, p=p):
                cp = pltpu.make_async_copy(src_at(pl.multiple_of(src + off, SEG_ALIGN), p),
                                           dst_at(pl.multiple_of(dst + off, SEG_ALIGN), p), sem)
                if start:
                    cp.start()
                else:
                    cp.wait()


def _ffn_moe_body(te_ref, parts_ref, nv_ref, xs_ref, w1_ref, w3_ref, w2_ref, o_ref, xb_ref, acc_ref):
    del te_ref
    i = pl.program_id(0)
    j = pl.program_id(1)
    valid = i < nv_ref[0]

    @pl.when(jnp.logical_and(valid, j == 0))
    def _():
        xb_ref[...] = xs_ref[...].astype(BF16)
        acc_ref[...] = jnp.zeros_like(acc_ref)

    for parts in range(1, MOE_PARTS + 1):
        @pl.when(parts_ref[i] == parts)
        def _(rows=parts * (T_MOE // MOE_PARTS)):
            h = xb_ref[:rows]
            a = jnp.dot(h, w1_ref[0, 0].astype(BF16), preferred_element_type=F32)
            b = jnp.dot(h, w3_ref[0, 0].astype(BF16), preferred_element_type=F32)
            t = (_silu(a) * b).astype(BF16)
            acc_ref[:rows] += jnp.dot(t, w2_ref[0, 0].astype(BF16), preferred_element_type=F32)

    last = j == pl.num_programs(1) - 1

    @pl.when(jnp.logical_and(valid, last))
    def _():
        o_ref[...] = acc_ref[...]

    @pl.when(jnp.logical_and(jnp.logical_not(valid), last))
    def _():
        o_ref[...] = jnp.zeros_like(o_ref)


def _ffn_moe(xs, tile_expert, tile_parts, n_valid, w1, w3, w2, layer):
    nj = D_FF // T_FF

    def jj(i, j, nv):
        return jnp.where(i < nv[0], j, nj - 1)

    grid_spec = pltpu.PrefetchScalarGridSpec(
        num_scalar_prefetch=3,
        grid=(xs.shape[0] // T_MOE, nj),
        in_specs=[
            pl.BlockSpec((T_MOE, D), lambda i, j, te, tp, nv: (i, 0)),
            pl.BlockSpec((1, 1, D, T_FF), lambda i, j, te, tp, nv: (layer, te[i], 0, jj(i, j, nv))),
            pl.BlockSpec((1, 1, D, T_FF), lambda i, j, te, tp, nv: (layer, te[i], 0, jj(i, j, nv))),
            pl.BlockSpec((1, 1, T_FF, D), lambda i, j, te, tp, nv: (layer, te[i], jj(i, j, nv), 0)),
        ],
        out_specs=pl.BlockSpec((T_MOE, D), lambda i, j, te, tp, nv: (i, 0)),
        scratch_shapes=[pltpu.VMEM((T_MOE, D), BF16), pltpu.VMEM((T_MOE, D), F32)],
    )
    return pl.pallas_call(
        _ffn_moe_body,
        grid_spec=grid_spec,
        out_shape=jax.ShapeDtypeStruct(xs.shape, F32),
        compiler_params=pltpu.CompilerParams(
            dimension_semantics=("arbitrary", "arbitrary"), vmem_limit_bytes=VMEM_BIG),
        name="ffn_moe",
    )(tile_expert, tile_parts, n_valid, xs, w1, w3, w2)


def _combine_body(start_ref, len_ref, s1_ref, s2_ref, w1_ref, w2_ref, x_ref, gate_ref, os_ref, o_ref,
                  lo_ref, sem):
    i = pl.program_id(0)
    last = pl.num_programs(0) - 1
    slot = i % 2
    other = 1 - slot
    os_at = lambda a, p: os_ref.at[pl.ds(a, p)]

    def gather(tile, buf, start, enable=True):
        plan = [(ln, far, near) for ln, near, far in _tile_plan(tile, len_ref, start_ref)]
        _segment_copies(plan, os_at, lambda a, p: lo_ref.at[buf, pl.ds(a, p)], sem.at[buf], start, enable)

    @pl.when(i == 0)
    def _():
        lo_ref[...] = jnp.zeros_like(lo_ref)
        gather(i, slot, start=True)

    gather(jnp.minimum(i + 1, last), other, start=True, enable=i < last)
    gather(i, slot, start=False)

    lo = lo_ref[slot].astype(BF16)
    lane = lax.broadcasted_iota(I32, (T_ROUTE, S_LOC), 1)
    pick1 = jnp.where(lane == s1_ref[...], 1.0, 0.0).astype(BF16)
    pick2 = jnp.where(lane == s2_ref[...], 1.0, 0.0).astype(BF16)
    y = (w1_ref[...] * jnp.dot(pick1, lo, preferred_element_type=F32)
         + w2_ref[...] * jnp.dot(pick2, lo, preferred_element_type=F32))
    o_ref[...] = x_ref[...] + gate_ref[0] * y


def _combine(x, n, gate, seg_start, seg_len, s1, s2, w1, w2, o_sorted):
    nt = n // T_ROUTE
    col = lambda a: a.reshape(n, 1)
    col_spec = pl.BlockSpec((T_ROUTE, 1), lambda i, *_: (i, 0))
    grid_spec = pltpu.PrefetchScalarGridSpec(
        num_scalar_prefetch=2,
        grid=(nt,),
        in_specs=[col_spec, col_spec, col_spec, col_spec,
                  pl.BlockSpec((T_ROUTE, D), lambda i, *_: (i, 0)),
                  _mod_spec(T_ROUTE),
                  pl.BlockSpec(memory_space=pl.ANY)],
        out_specs=pl.BlockSpec((T_ROUTE, D), lambda i, *_: (i, 0)),
        scratch_shapes=[pltpu.VMEM((2, S_LOC, D), F32), pltpu.SemaphoreType.DMA((2,))],
    )
    return pl.pallas_call(
        _combine_body,
        grid_spec=grid_spec,
        out_shape=jax.ShapeDtypeStruct((n, D), F32),
        compiler_params=pltpu.CompilerParams(
            dimension_semantics=("arbitrary",), vmem_limit_bytes=VMEM_BIG),
        name="moe_combine",
    )(seg_start.reshape(-1), seg_len.reshape(-1), col(s1), col(s2), col(w1), col(w2), x, gate, o_sorted)


def _moe(x, n, sh, sc, gate, ng, router, w1, w3, w2, layer):
    nt = n // T_ROUTE
    s1, s2, p1, p2, pc = _router(x, n, sh, sc, ng, router)
    seg_len = pc[:, :, 0]
    total = jnp.sum(seg_len, axis=0)
    tiles_per = (total + T_MOE - 1) // T_MOE
    tile_ends = jnp.cumsum(tiles_per)
    group_start = (tile_ends - tiles_per) * T_MOE
    seg_start = group_start[None, :] + jnp.cumsum(seg_len, axis=0) - seg_len
    tail_start = group_start + total
    tail_len = tiles_per * T_MOE - total
    n_valid = tile_ends[-1:]
    n_tiles = -(-(2 * n + nt * N_EXPERTS * (SEG_ALIGN - 1)) // T_MOE) + N_EXPERTS
    tile_all = jnp.arange(n_tiles, dtype=I32)
    tile_ids = jnp.minimum(tile_all, n_valid - 1)
    tile_expert = jnp.sum((tile_ids[:, None] >= tile_ends[None, :]).astype(I32), axis=1)
    tile_rows = jnp.clip((group_start + total)[tile_expert] - tile_ids * T_MOE, 0, T_MOE)
    part = T_MOE // MOE_PARTS
    tile_parts = jnp.where(tile_all < n_valid, (tile_rows + part - 1) // part, 0).astype(I32)
    xs = _dispatch(x, n, sh, sc, ng, s1, s2, seg_start, seg_len, tail_start, tail_len, n_valid, n_tiles)
    o_sorted = _ffn_moe(xs, tile_expert, tile_parts, n_valid, w1, w3, w2, layer)
    return _combine(x, n, gate, seg_start, seg_len, s1, s2, p1, p2, o_sorted)


def _swap_halves(y):
    q = HEAD_DIM // 4
    return jnp.concatenate([y[q:2 * q], y[:q], y[3 * q:], y[2 * q:3 * q]], axis=0)


def _norm_rope_t(zh, c_tab, s_tab):
    ss = jnp.mean(zh * zh, axis=0, keepdims=True)
    yn = zh * lax.rsqrt(ss + EPS)
    return yn * c_tab + _swap_halves(yn) * s_tab


def _qkv_body(x_ref, sh_ref, sc_ref, ng_ref, w_ref, cq_ref, sq_ref, ck_ref, sk_ref, q_ref, k_ref, v_ref):
    h = _modnorm(x_ref[...], ng_ref[...], sc_ref[0], sh_ref[0]).astype(BF16)
    zt = lax.dot_general(w_ref[...], h, (((1,), (1,)), ((), ())), preferred_element_type=F32)
    nq = N_HEADS * HEAD_DIM
    nkv = N_KV * HEAD_DIM
    cq, sq, ck, sk = cq_ref[...], sq_ref[...], ck_ref[...], sk_ref[...]
    for hd in range(N_HEADS):
        lo = hd * HEAD_DIM
        q_ref[lo:lo + HEAD_DIM, :] = _norm_rope_t(zt[lo:lo + HEAD_DIM], cq, sq).astype(BF16)
    kt = jnp.concatenate(
        [_norm_rope_t(zt[nq + kh * HEAD_DIM:nq + (kh + 1) * HEAD_DIM], ck, sk) for kh in range(N_KV)],
        axis=0)
    k_ref[...] = kt.T.astype(BF16)
    v_ref[...] = zt[nq + nkv:].astype(BF16)


def _rope_tables(q_g, k_g):
    half = HEAD_DIM // 2
    quarter = HEAD_DIM // 4
    pos = jnp.arange(SEQ)
    pos_row = (pos // GRID_W).astype(F32)
    pos_col = (pos % GRID_W).astype(F32)
    inv_freq = ROPE_THETA ** (-jnp.arange(0, half, 2, dtype=F32) / half)
    ang_row = inv_freq[:, None] * pos_row[None, :]
    ang_col = inv_freq[:, None] * pos_col[None, :]
    ang = jnp.concatenate([ang_row, ang_row, ang_col, ang_col], axis=0)
    cos = jnp.concatenate([jnp.cos(ang), jnp.ones((HEAD_DIM, T_QKV), F32)], axis=1)
    sin = jnp.concatenate([jnp.sin(ang), jnp.zeros((HEAD_DIM, T_QKV), F32)], axis=1)
    first = ((jnp.arange(HEAD_DIM) % half) < quarter)[:, None]
    sin = jnp.where(first, -sin, sin)

    def tables(g, scale):
        g = g.astype(F32) * scale
        partner = _swap_halves(g[:, None])
        return g[:, None] * cos, partner * sin

    return tables(q_g, HEAD_DIM ** -0.5 * LOG2_E) + tables(k_g, 1.0)


def _qkv(x, sh, sc, ng, w_qkv, q_g, k_g):
    n = x.shape[0]
    nt = n // T_QKV
    nkv = N_KV * HEAD_DIM
    wd = w_qkv.shape[1]
    per_seq = SEQ // T_QKV
    tab_spec = pl.BlockSpec(
        (HEAD_DIM, T_QKV), lambda i: (0, jnp.where(i < N_LAT // T_QKV, i % per_seq, per_seq)))
    mod = _mod_spec(T_QKV)
    return pl.pallas_call(
        _qkv_body,
        grid=(nt,),
        in_specs=[
            pl.BlockSpec((T_QKV, D), lambda i: (i, 0)),
            mod, mod,
            _const_spec((1, D)),
            _const_spec((wd, D)),
            tab_spec, tab_spec, tab_spec, tab_spec,
        ],
        out_specs=[pl.BlockSpec((D, T_QKV), lambda i: (0, i)),
                   pl.BlockSpec((T_QKV, nkv), lambda i: (i, 0)),
                   pl.BlockSpec((nkv, T_QKV), lambda i: (0, i))],
        out_shape=[jax.ShapeDtypeStruct((D, n), BF16),
                   jax.ShapeDtypeStruct((n, nkv), BF16),
                   jax.ShapeDtypeStruct((nkv, n), BF16)],
        compiler_params=pltpu.CompilerParams(
            dimension_semantics=("arbitrary",), vmem_limit_bytes=VMEM_BIG),
        name="attn_qkv",
    )(x, sh, sc, ng.reshape(1, D), w_qkv.T.astype(BF16), *_rope_tables(q_g, k_g))


def _attn_heads(qt_ref, k_all, vt_all, band, sink_ref, o_ref):
    group = N_HEADS // N_KV
    nk = k_all.shape[0]
    ones = jnp.ones((BF16_ROWS, nk), BF16)
    zeros = jnp.zeros((HEAD_DIM, group * T_Q), BF16)
    outs = []
    scores = []
    probs = []
    for kh in range(N_KV):
        q4 = jnp.concatenate(
            [qt_ref[(kh * group + g) * HEAD_DIM:(kh * group + g + 1) * HEAD_DIM, :] for g in range(group)],
            axis=1)
        qpad = jnp.concatenate([q4, zeros] if kh % 2 == 0 else [zeros, q4], axis=0)
        k2 = k_all[:, (kh // 2) * LANES:(kh // 2 + 1) * LANES]
        scores.append(jnp.dot(k2, qpad, preferred_element_type=F32))
    for kh in range(N_KV):
        st = scores[kh]
        blocks = [st[c * T_Q:(c + 1) * T_Q] for c in range(nk // T_Q)]
        if band is not None:
            blocks[0] = jnp.where(band[0], blocks[0], NEG_INF)
            blocks[2] = jnp.where(band[1], blocks[2], NEG_INF)
        best = blocks[0]
        for blk in blocks[1:]:
            best = jnp.maximum(best, blk)
        sink = jnp.concatenate(
            [jnp.full((1, T_Q), sink_ref[kh * group + g] * LOG2_E, F32) for g in range(group)], axis=1)
        m = jnp.maximum(jnp.max(best, axis=0, keepdims=True), sink)
        pt = jnp.concatenate([jnp.exp2(blk - m).astype(BF16) for blk in blocks], axis=0)
        probs.append((pt, jnp.exp2(sink - m)))
    for kh in range(N_KV):
        pt, sink_term = probs[kh]
        vt_aug = jnp.concatenate([vt_all[kh * HEAD_DIM:(kh + 1) * HEAD_DIM, :], ones], axis=0)
        ot = jnp.dot(vt_aug, pt, preferred_element_type=F32)
        den = ot[HEAD_DIM:HEAD_DIM + 1] + sink_term
        o = ot[:HEAD_DIM] / den
        outs.extend(o[:, g * T_Q:(g + 1) * T_Q] for g in range(group))
    o_ref[...] = jnp.concatenate(outs, axis=0).T.astype(BF16)


def _attn_body(sink_ref, q_ref, kp_ref, kc_ref, kn_ref, kx_ref, vp_ref, vc_ref, vn_ref, vx_ref, o_ref,
               *, n_q_blocks):
    iq = pl.program_id(1)
    per_seq = SEQ // T_Q
    group = N_HEADS // N_KV

    @pl.when(iq < per_seq)
    def _():
        k_all = jnp.concatenate([kp_ref[...], kc_ref[...], kn_ref[...], kx_ref[...]], axis=0)
        vt_all = jnp.concatenate([vp_ref[...], vc_ref[...], vn_ref[...], vx_ref[...]], axis=1)
        c = lax.broadcasted_iota(I32, (T_Q, group * T_Q), 0)
        r = lax.broadcasted_iota(I32, (T_Q, group * T_Q), 1) % T_Q
        far = 2 * T_Q
        mask_prev = c >= r + jnp.where(iq > 0, 0, far)
        mask_next = c <= r - jnp.where(iq < per_seq - 1, 0, far)
        _attn_heads(q_ref, k_all, vt_all, (mask_prev, mask_next), sink_ref, o_ref)

    if n_q_blocks > per_seq:
        @pl.when(iq >= per_seq)
        def _():
            _attn_heads(q_ref, kx_ref[...], vx_ref[...], None, sink_ref, o_ref)


def _attention(qt, k, vt, sink, need_ctx):
    assert T_Q == WINDOW
    per_seq = SEQ // T_Q
    ctx_blocks = CTX_LEN // T_Q
    n_q_blocks = per_seq + (ctx_blocks if need_ctx else 0)
    n_out = N_ALL if need_ctx else N_LAT
    lat_blocks = N_LAT // T_Q
    kw = k.shape[1]

    def q_blk(b, iq):
        return jnp.where(iq < per_seq, b * per_seq + iq, lat_blocks + b * ctx_blocks + (iq - per_seq))

    def win_blk(b, iq, off):
        return b * per_seq + jnp.clip(iq + off, 0, per_seq - 1)

    ctx_blk = lambda b: N_LAT // CTX_LEN + b
    k_win = lambda off: pl.BlockSpec((T_Q, kw), lambda b, iq, *_: (win_blk(b, iq, off), 0))
    v_win = lambda off: pl.BlockSpec((kw, T_Q), lambda b, iq, *_: (0, win_blk(b, iq, off)))
    k_ctx = pl.BlockSpec((CTX_LEN, kw), lambda b, iq, *_: (ctx_blk(b), 0))
    v_ctx = pl.BlockSpec((kw, CTX_LEN), lambda b, iq, *_: (0, ctx_blk(b)))
    grid_spec = pltpu.PrefetchScalarGridSpec(
        num_scalar_prefetch=1,
        grid=(NB, n_q_blocks),
        in_specs=[pl.BlockSpec((D, T_Q), lambda b, iq, *_: (0, q_blk(b, iq))),
                  k_win(-1), k_win(0), k_win(1), k_ctx, v_win(-1), v_win(0), v_win(1), v_ctx],
        out_specs=pl.BlockSpec((T_Q, D), lambda b, iq, *_: (q_blk(b, iq), 0)),
    )
    return pl.pallas_call(
        functools.partial(_attn_body, n_q_blocks=n_q_blocks),
        grid_spec=grid_spec,
        out_shape=jax.ShapeDtypeStruct((n_out, D), BF16),
        compiler_params=pltpu.CompilerParams(
            dimension_semantics=("arbitrary", "arbitrary"), vmem_limit_bytes=VMEM_BIG),
        name="attn_core",
    )(sink.astype(F32), qt, k, k, k, k, vt, vt, vt, vt)


def _proj_body(a_ref, x_ref, gate_ref, w_ref, o_ref):
    y = jnp.dot(a_ref[...], w_ref[...], preferred_element_type=F32)
    o_ref[...] = x_ref[...] + gate_ref[0] * y


def _proj_residual(a, x, gate, w):
    n = a.shape[0]
    return pl.pallas_call(
        _proj_body,
        grid=(n // T_PROJ,),
        in_specs=[pl.BlockSpec((T_PROJ, D), lambda i: (i, 0)),
                  pl.BlockSpec((T_PROJ, D), lambda i: (i, 0)),
                  _mod_spec(T_PROJ),
                  _const_spec((D, D))],
        out_specs=pl.BlockSpec((T_PROJ, D), lambda i: (i, 0)),
        out_shape=jax.ShapeDtypeStruct((n, D), F32),
        compiler_params=pltpu.CompilerParams(dimension_semantics=("arbitrary",)),
        name="attn_proj",
    )(a, x, gate, w.astype(BF16))


def kernel(x, c, ctx, c_ctx, ada_w, ada_b, norm_mix_g, norm_ffn_g, ev_w_in, ev_ln_g, ev_ln_b, ev_ws,
           ev_bs, ev_conv_w, ev_conv_b, ev_cnorm_g, ev_w_out, od_w_qkv, od_q_g, od_k_g, od_sink, od_w_o,
           ff_w1, ff_w3, ff_w2, moe_router, moe_w1, moe_w3, moe_w2):
    assert x.shape == (NB, SEQ, D) and ctx.shape == (NB, CTX_LEN, D)
    mods = _ada_mods(c, c_ctx, ada_w, ada_b)
    xa = jnp.concatenate([x.reshape(N_LAT, D), ctx.reshape(N_CTX, D)], axis=0)
    ev_w_in_b, ev_w_out_b = ev_w_in.astype(BF16), ev_w_out.astype(BF16)
    for li in range(DEPTH):
        need_ctx = li < DEPTH - 1
        j = li // 2
        sh1, sc1, g1, sh2, sc2, g2 = mods[li]
        if li % 2 == 0:
            xa = _even_mixer(xa, sh1, sc1, g1, norm_mix_g[li], ev_w_in_b, ev_ln_g[j], ev_ln_b[j],
                             ev_ws[j], ev_bs[j], ev_conv_w[j], ev_conv_b[j], ev_cnorm_g[j], ev_w_out_b, j)
            xa = _ffn_dense(xa, sh2, sc2, g2, norm_ffn_g[li], ff_w1, ff_w3, ff_w2, j)
        else:
            qt, k, vt = _qkv(xa, sh1, sc1, norm_mix_g[li], od_w_qkv[j], od_q_g[j], od_k_g[j])
            o = _attention(qt, k, vt, od_sink[j], need_ctx)
            xa = _proj_residual(o, xa, g1, od_w_o[j])
            n = N_ALL if need_ctx else N_LAT
            xa = _moe(xa, n, sh2, sc2, g2, norm_ffn_g[li], moe_router[j], moe_w1, moe_w3, moe_w2, j)
    return xa[:N_LAT].reshape(NB, SEQ, D)
```

```python
import functools

import jax
import jax.numpy as jnp
from jax import lax
from jax.experimental import pallas as pl
from jax.experimental.pallas import tpu as pltpu

F32 = jnp.float32
BF16 = jnp.bfloat16
I32 = jnp.int32

D = 1024
NB = 8
SEQ = 2048
CTX_LEN = 256
DEPTH = 4
GRID_W = 64
CHUNK = 128
A_GROUPS = 8
CONV_W = 31
N_HEADS = 16
N_KV = 4
HEAD_DIM = 64
WINDOW = 128
ROPE_THETA = 10000.0
D_FF = 3584
N_EXPERTS = 8
EPS = 1e-6
NEG_INF = -1e30
LOG2_E = 1.4426950408889634

N_LAT = NB * SEQ
N_CTX = NB * CTX_LEN
N_ALL = N_LAT + N_CTX
CTX_MOD_ROW = NB

LANES = 128
SUBLANES = 8
BF16_ROWS = 16

T_EVEN = 256
EVEN_PAIR = 2
HALO = BF16_ROWS
T_FFN = 1024
T_FF = 512
T_ROUTE = 512
T_MOE = 1024
MOE_PARTS = 4
SEG_ALIGN = SUBLANES
S_LOC = 2 * T_ROUTE + N_EXPERTS * SEG_ALIGN
SEG_SIZES = tuple(T_ROUTE >> s for s in range(T_ROUTE.bit_length()) if T_ROUTE >> s >= SEG_ALIGN)
assert SEG_SIZES[-1] == SEG_ALIGN
T_QKV = 256
T_Q = 128
T_PROJ = 512

VMEM_BIG = 52 * 1024 * 1024


def _mod_spec(tile):
    per = SEQ // tile
    return pl.BlockSpec((1, 1, D), lambda i, *_: (jnp.minimum(i // per, CTX_MOD_ROW), 0, 0))


def _const_spec(shape):
    nd = len(shape)
    return pl.BlockSpec(shape, lambda *_: (0,) * nd)


def _modnorm(x, g, sc, sh):
    ms = jnp.mean(x * x, axis=-1, keepdims=True)
    return (x * lax.rsqrt(ms + EPS) * g) * (1.0 + sc) + sh


def _silu(x):
    return x * jax.nn.sigmoid(x)


def _ada_body(c_ref, w_ref, b_ref, o_ref):
    a = _silu(c_ref[...])
    o_ref[0] = jnp.dot(a.astype(BF16), w_ref[0].astype(BF16), preferred_element_type=F32) + b_ref[0]


def _ada_mods(c, c_ctx, ada_w, ada_b):
    rows = 16
    cc = jnp.concatenate([c, c_ctx[None, :], jnp.zeros((rows - NB - 1, D), F32)], axis=0)
    out = pl.pallas_call(
        _ada_body,
        grid=(DEPTH, 6),
        in_specs=[
            pl.BlockSpec((rows, D), lambda l, n: (0, 0)),
            pl.BlockSpec((1, D, D), lambda l, n: (l, 0, n)),
            pl.BlockSpec((1, 1, D), lambda l, n: (l, 0, n)),
        ],
        out_specs=pl.BlockSpec((1, rows, D), lambda l, n: (l, 0, n)),
        out_shape=jax.ShapeDtypeStruct((DEPTH, rows, 6 * D), F32),
        name="ada_mod",
    )(cc, ada_w, ada_b.reshape(DEPTH, 1, 6 * D))
    m = out[:, :NB + 1].reshape(DEPTH, NB + 1, 6, 1, D)
    return [[m[l, :, k] for k in range(6)] for l in range(DEPTH)]


def _even_body(x_ref, xc_ref, xp_ref, xn_ref, sh_ref, sc_ref, gate_ref, ng_ref, win_ref, lng_ref, lnb_ref,
               ws_ref, bs_ref, cw_ref, cb_ref, cng_ref, wout_ref, o_ref, gext_ref, *, split):
    step = pl.program_id(0)
    per_seq = SEQ // T_EVEN
    ext = T_EVEN + 2 * HALO
    halves = range(EVEN_PAIR)
    x_all = x_ref[...]
    if split:
        x_all = jnp.where(step < N_LAT // (EVEN_PAIR * T_EVEN), x_all, xc_ref[...])
    tiles = [x_all[hf * T_EVEN:(hf + 1) * T_EVEN] for hf in halves]
    befores = [xp_ref[...]] + [tiles[hf - 1][T_EVEN - HALO:] for hf in halves[1:]]
    afters = [tiles[hf + 1][:HALO] for hf in halves[:-1]] + [xn_ref[...]]

    hs = [_modnorm(jnp.concatenate([befores[hf], tiles[hf], afters[hf]], axis=0),
                   ng_ref[...], sc_ref[0], sh_ref[0]).astype(BF16) for hf in halves]
    zbs = [jnp.dot(hs[hf], win_ref[0, :, 2 * D:], preferred_element_type=F32) for hf in halves]
    zas = [jnp.dot(hs[hf][HALO:HALO + T_EVEN], win_ref[0, :, :2 * D], preferred_element_type=F32)
           for hf in halves]

    for hf in halves:
        i = step * EVEN_PAIR + hf
        is_lat = i < N_LAT // T_EVEN
        pos = i % per_seq
        is_start = jnp.logical_or(jnp.logical_not(is_lat), pos == 0)
        is_end = jnp.logical_or(jnp.logical_not(is_lat), pos == per_seq - 1)
        gg = zbs[hf][:, :D] * jax.nn.sigmoid(zbs[hf][:, D:])
        row = lax.broadcasted_iota(I32, (ext, 1), 0)
        lo = jnp.where(is_start, HALO, 0)
        hi = jnp.where(is_end, T_EVEN + HALO, ext)
        gg = jnp.where(jnp.logical_and(row >= lo, row < hi), gg, 0.0)
        for cbk in range(D // LANES):
            gext_ref[hf, cbk] = gg[:, cbk * LANES:(cbk + 1) * LANES]

    rows_per = 64
    gdim = D // A_GROUPS
    for hf in halves:
        cols = []
        for cbk in range(D // LANES):
            blocks = []
            for rb in range(T_EVEN // rows_per):
                acc = jnp.zeros((rows_per, LANES), F32)
                for k in range(CONV_W):
                    off = rb * rows_per + k + HALO - CONV_W // 2
                    acc = acc + cw_ref[cbk, pl.ds(k, 1), :] * gext_ref[hf, cbk, pl.ds(off, rows_per), :]
                blocks.append(acc)
            cols.append(jnp.concatenate(blocks, axis=0))

        cv = jnp.concatenate(cols, axis=1) + cb_ref[...]
        ms = jnp.mean(cv * cv, axis=-1, keepdims=True)
        yb = _silu(cv * lax.rsqrt(ms + EPS) * cng_ref[...])

        u = jax.nn.gelu(zas[hf][:, :D])
        v = jax.nn.gelu(zas[hf][:, D:])
        mu = jnp.mean(v, axis=-1, keepdims=True)
        vc = v - mu
        var = jnp.mean(vc * vc, axis=-1, keepdims=True)
        vn = (vc * lax.rsqrt(var + EPS) * lng_ref[...] + lnb_ref[...]).astype(BF16)
        chunks = []
        for ck in range(T_EVEN // CHUNK):
            blocks = [
                jnp.dot(ws_ref[g], vn[ck * CHUNK:(ck + 1) * CHUNK, g * gdim:(g + 1) * gdim],
                        preferred_element_type=F32)
                for g in range(A_GROUPS)
            ]
            chunks.append(jnp.concatenate(blocks, axis=1) + bs_ref[...])
        ya = u * jnp.concatenate(chunks, axis=0)

        y = (jnp.dot(ya.astype(BF16), wout_ref[0, :D], preferred_element_type=F32)
             + jnp.dot(yb.astype(BF16), wout_ref[0, D:], preferred_element_type=F32))
        o_ref[hf * T_EVEN:(hf + 1) * T_EVEN, :] = tiles[hf] + gate_ref[0] * y


def _even_mixer(x, x_ctx, sh, sc, gate, ng, w_in, ln_g, ln_b, ws, bs, conv_w, conv_b, cn_g, w_out, j):
    split = x_ctx is not None
    n = x.shape[0] + (x_ctx.shape[0] if split else 0)
    rows = EVEN_PAIR * T_EVEN
    nt = n // rows
    hb = rows // HALO
    last = x.shape[0] // HALO - 1
    x_steps = x.shape[0] // rows
    if split:
        main_specs = [pl.BlockSpec((rows, D), lambda i: (jnp.minimum(i, x_steps - 1), 0)),
                      pl.BlockSpec((rows, D), lambda i: (jnp.maximum(i - x_steps, 0), 0))]
        x_second = x_ctx
    else:
        main_specs = [pl.BlockSpec((rows, D), lambda i: (i, 0)), _const_spec((HALO, D))]
        x_second = x[:HALO]
    ncb = D // LANES
    bs_full = jnp.repeat(bs.T, D // A_GROUPS, axis=1)
    cw = jnp.pad(conv_w, ((0, 32 - CONV_W), (0, 0))).reshape(32, ncb, LANES).transpose(1, 0, 2)
    mod = _mod_spec(rows)
    row1 = lambda a: a.reshape(1, D)
    return pl.pallas_call(
        functools.partial(_even_body, split=split),
        grid=(nt,),
        in_specs=main_specs + [
            pl.BlockSpec((HALO, D), lambda i: (jnp.clip(i * hb - 1, 0, last), 0)),
            pl.BlockSpec((HALO, D), lambda i: (jnp.minimum((i + 1) * hb, last), 0)),
            mod, mod, mod,
            _const_spec((1, D)),
            pl.BlockSpec((1, D, 4 * D), lambda i: (j, 0, 0)),
            _const_spec((1, D)), _const_spec((1, D)),
            _const_spec((A_GROUPS, CHUNK, CHUNK)),
            _const_spec((CHUNK, D)),
            _const_spec((ncb, 32, LANES)),
            _const_spec((1, D)), _const_spec((1, D)),
            pl.BlockSpec((1, 2 * D, D), lambda i: (j, 0, 0)),
        ],
        out_specs=pl.BlockSpec((rows, D), lambda i: (i, 0)),
        out_shape=jax.ShapeDtypeStruct((n, D), F32),
        scratch_shapes=[
            pltpu.VMEM((EVEN_PAIR, ncb, T_EVEN + 2 * HALO, LANES), F32),
        ],
        compiler_params=pltpu.CompilerParams(
            dimension_semantics=("arbitrary",), vmem_limit_bytes=VMEM_BIG),
        name="even_mixer",
    )(x, x_second, x, x, sh, sc, gate, row1(ng), w_in, row1(ln_g), row1(ln_b), ws.astype(BF16),
      bs_full, cw, row1(conv_b), row1(cn_g), w_out)


def _ffn_dense_body(x_ref, sh_ref, sc_ref, gate_ref, ng_ref, w1_ref, w3_ref, w2_ref, o_ref,
                    h_ref, acc_ref):
    j = pl.program_id(1)

    @pl.when(jnp.logical_and(pl.program_id(0) == 0, j == 0))
    def _():
        acc_ref[...] = jnp.zeros_like(acc_ref)

    @pl.when(j == 0)
    def _():
        h_ref[...] = _modnorm(x_ref[...], ng_ref[...], sc_ref[0], sh_ref[0]).astype(BF16)

    h = h_ref[...]
    a = jnp.dot(h, w1_ref[0].astype(BF16), preferred_element_type=F32)
    b = jnp.dot(h, w3_ref[0].astype(BF16), preferred_element_type=F32)
    t = (_silu(a) * b).astype(BF16)
    prev = jnp.where(j == 0, 0.0, acc_ref[...])
    acc_ref[...] = prev + jnp.dot(t, w2_ref[0].astype(BF16), preferred_element_type=F32)

    @pl.when(j == pl.num_programs(1) - 1)
    def _():
        o_ref[...] = x_ref[...] + gate_ref[0] * acc_ref[...]


def _ffn_dense(x, sh, sc, gate, ng, w1, w3, w2, layer):
    n = x.shape[0]
    mod = _mod_spec(T_FFN)
    return pl.pallas_call(
        _ffn_dense_body,
        grid=(n // T_FFN, D_FF // T_FF),
        in_specs=[
            pl.BlockSpec((T_FFN, D), lambda i, j: (i, 0)),
            mod, mod, mod,
            _const_spec((1, D)),
            pl.BlockSpec((1, D, T_FF), lambda i, j: (layer, 0, j)),
            pl.BlockSpec((1, D, T_FF), lambda i, j: (layer, 0, j)),
            pl.BlockSpec((1, T_FF, D), lambda i, j: (layer, j, 0)),
        ],
        out_specs=pl.BlockSpec((T_FFN, D), lambda i, j: (i, 0)),
        out_shape=jax.ShapeDtypeStruct((n, D), F32),
        scratch_shapes=[pltpu.VMEM((T_FFN, D), BF16), pltpu.VMEM((T_FFN, D), F32)],
        compiler_params=pltpu.CompilerParams(
            dimension_semantics=("arbitrary", "arbitrary"), vmem_limit_bytes=VMEM_BIG),
        name="ffn_dense",
    )(x, sh, sc, gate, ng.reshape(1, D), w1, w3, w2)


def _router_body(x_ref, sh_ref, sc_ref, ng_ref, rt_ref, s1_ref, s2_ref, w1_ref, w2_ref, pc_ref):
    h = _modnorm(x_ref[...], ng_ref[...], sc_ref[0], sh_ref[0])
    lg = lax.dot_general(rt_ref[...], h, (((1,), (1,)), ((), ())),
                         precision=lax.Precision.HIGHEST, preferred_element_type=F32)
    e = lax.broadcasted_iota(I32, lg.shape, 0).astype(F32)
    big = float(N_EXPERTS)
    m1 = jnp.max(lg, axis=0, keepdims=True)
    i1 = jnp.min(jnp.where(lg == m1, e, big), axis=0, keepdims=True)
    lg2 = jnp.where(e == i1, -jnp.inf, lg)
    m2 = jnp.max(lg2, axis=0, keepdims=True)
    i2 = jnp.min(jnp.where(lg2 == m2, e, big), axis=0, keepdims=True)
    e2 = jnp.exp(m2 - m1)
    den = 1.0 + e2
    w1_ref[0] = 1.0 / den
    w2_ref[0] = e2 / den

    sel1 = e == i1
    sel2 = e == i2
    member = jnp.where(jnp.logical_or(sel1, sel2), 1.0, 0.0)
    t = lg.shape[1]
    before = (lax.broadcasted_iota(I32, (t, t), 0) < lax.broadcasted_iota(I32, (t, t), 1))
    tri = jnp.where(before, 1.0, 0.0).astype(BF16)
    rank = jnp.dot(member.astype(BF16), tri, preferred_element_type=F32)
    count = jnp.sum(member, axis=1, keepdims=True)
    padded = jnp.ceil(count * (1.0 / SEG_ALIGN)) * SEG_ALIGN
    e_col = lax.broadcasted_iota(I32, (N_EXPERTS, 1), 0)
    seg_start = jnp.zeros((N_EXPERTS, 1), F32)
    for ex in range(1, N_EXPERTS):
        below = jnp.sum(jnp.where(e_col < ex, padded, 0.0), axis=0, keepdims=True)
        seg_start = jnp.where(e_col == ex, below, seg_start)
    slot = rank + seg_start
    s1 = jnp.sum(jnp.where(sel1, slot, 0.0), axis=0, keepdims=True).astype(I32)
    s2 = jnp.sum(jnp.where(sel2, slot, 0.0), axis=0, keepdims=True).astype(I32)
    s1_ref[0] = s1
    s2_ref[0] = s2
    pc_ref[0] = jnp.broadcast_to(padded.astype(I32), (N_EXPERTS, LANES))


def _router(x, n, sh, sc, ng, router):
    nt = n // T_ROUTE
    mod = _mod_spec(T_ROUTE)
    vec_spec = pl.BlockSpec((1, 1, T_ROUTE), lambda i: (i, 0, 0))
    vec_i = jax.ShapeDtypeStruct((nt, 1, T_ROUTE), I32)
    vec_f = jax.ShapeDtypeStruct((nt, 1, T_ROUTE), F32)
    return pl.pallas_call(
        _router_body,
        grid=(nt,),
        in_specs=[
            pl.BlockSpec((T_ROUTE, D), lambda i: (i, 0)),
            mod, mod,
            _const_spec((1, D)),
            _const_spec((N_EXPERTS, D)),
        ],
        out_specs=[vec_spec, vec_spec, vec_spec, vec_spec,
                   pl.BlockSpec((1, N_EXPERTS, LANES), lambda i: (i, 0, 0))],
        out_shape=[vec_i, vec_i, vec_f, vec_f, jax.ShapeDtypeStruct((nt, N_EXPERTS, LANES), I32)],
        compiler_params=pltpu.CompilerParams(dimension_semantics=("arbitrary",)),
        name="moe_router",
    )(x, sh, sc, ng.reshape(1, D), router.T)


def _dispatch_body(start_ref, len_ref, tail_start_ref, tail_len_ref, nv_ref,
                   x_ref, sh_ref, sc_ref, ng_ref, s1_ref, s2_ref, xs_ref, lbuf_ref, zbuf_ref, sem,
                   *, n_tiles, min_tiles):
    i = pl.program_id(0)
    slot = i % 2
    other = 1 - slot
    h = _modnorm(x_ref[...], ng_ref[...], sc_ref[0], sh_ref[0]).astype(BF16)
    rows = lax.broadcasted_iota(I32, (S_LOC, T_ROUTE), 0)
    perm = jnp.where(jnp.logical_or(rows == s1_ref[0], rows == s2_ref[0]), 1.0, 0.0).astype(BF16)
    lbuf_ref[slot] = jnp.dot(perm, h, preferred_element_type=F32)

    xs_at = lambda a, p: xs_ref.at[pl.ds(a, p)]
    _segment_copies(_tile_plan(i, len_ref, start_ref), lambda a, p: lbuf_ref.at[slot, pl.ds(a, p)], xs_at,
                    sem.at[slot], start=True)
    _segment_copies(_tile_plan(jnp.maximum(i - 1, 0), len_ref, start_ref),
                    lambda a, p: lbuf_ref.at[other, pl.ds(a, p)], xs_at, sem.at[other], start=False,
                    enable=i > 0)

    @pl.when(i == pl.num_programs(0) - 1)
    def _():
        _segment_copies(_tile_plan(i, len_ref, start_ref), lambda a, p: lbuf_ref.at[slot, pl.ds(a, p)],
                        xs_at, sem.at[slot], start=False)
        zbuf_ref[...] = jnp.zeros_like(zbuf_ref)
        tails = [(tail_len_ref[ex], 0, tail_start_ref[ex]) for ex in range(N_EXPERTS)]
        zero_at = lambda a, p: zbuf_ref.at[pl.ds(a, p)]
        _segment_copies(tails, zero_at, xs_at, sem.at[slot], start=True)
        _segment_copies(tails, zero_at, xs_at, sem.at[slot], start=False)
        for tile in range(min_tiles, n_tiles):
            @pl.when(tile >= nv_ref[0])
            def _(tile=tile):
                cp = pltpu.make_async_copy(zbuf_ref, xs_ref.at[pl.ds(tile * T_MOE, T_MOE)], sem.at[slot])
                cp.start()
                cp.wait()


def _dispatch(x, n, sh, sc, ng, s1, s2, seg_start, seg_len, tail_start, tail_len, n_valid, n_tiles):
    nt = n // T_ROUTE
    mod = _mod_spec(T_ROUTE)
    vec_spec = pl.BlockSpec((1, 1, T_ROUTE), lambda i, *_: (i, 0, 0))
    grid_spec = pltpu.PrefetchScalarGridSpec(
        num_scalar_prefetch=5,
        grid=(nt,),
        in_specs=[pl.BlockSpec((T_ROUTE, D), lambda i, *_: (i, 0)),
                  mod, mod,
                  pl.BlockSpec((1, D), lambda i, *_: (0, 0)),
                  vec_spec, vec_spec],
        out_specs=pl.BlockSpec(memory_space=pl.ANY),
        scratch_shapes=[pltpu.VMEM((2, S_LOC, D), F32), pltpu.VMEM((T_MOE, D), F32),
                        pltpu.SemaphoreType.DMA((2,))],
    )
    return pl.pallas_call(
        functools.partial(_dispatch_body, n_tiles=n_tiles, min_tiles=(2 * n) // T_MOE),
        grid_spec=grid_spec,
        out_shape=jax.ShapeDtypeStruct((n_tiles * T_MOE, D), F32),
        compiler_params=pltpu.CompilerParams(
            dimension_semantics=("arbitrary",), vmem_limit_bytes=VMEM_BIG),
        name="moe_dispatch",
    )(seg_start.reshape(-1), seg_len.reshape(-1), tail_start, tail_len, n_valid,
      x, sh, sc, ng.reshape(1, D), s1, s2)


def _tile_plan(tile, len_ref, far_ref):
    near = 0
    plan = []
    for ex in range(N_EXPERTS):
        ln = len_ref[tile * N_EXPERTS + ex]
        plan.append((ln, near, far_ref[tile * N_EXPERTS + ex]))
        near = near + ln
    return plan


def _segment_copies(plan, src_at, dst_at, sem, start, enable=True):
    for ln, src, dst in plan:
        for p in SEG_SIZES:
            off = jnp.bitwise_and(ln, -(2 * p))

            @pl.when(jnp.logical_and(jnp.bitwise_and(ln, p) != 0, enable))
            def _(off=off, src=src, dst=dst, p=p):
                cp = pltpu.make_async_copy(src_at(pl.multiple_of(src + off, SEG_ALIGN), p),
                                           dst_at(pl.multiple_of(dst + off, SEG_ALIGN), p), sem)
                if start:
                    cp.start()
                else:
                    cp.wait()


def _ffn_moe_body(te_ref, parts_ref, nv_ref, xs_ref, w1_ref, w3_ref, w2_ref, o_ref, acc_ref):
    del te_ref
    i = pl.program_id(0)
    j = pl.program_id(1)
    valid = i < nv_ref[0]

    @pl.when(jnp.logical_and(i == 0, j == 0))
    def _():
        acc_ref[...] = jnp.zeros_like(acc_ref)

    for parts in range(1, MOE_PARTS + 1):
        @pl.when(parts_ref[i] == parts)
        def _(rows=parts * (T_MOE // MOE_PARTS)):
            h = xs_ref[:rows].astype(BF16)
            a = jnp.dot(h, w1_ref[0, 0].astype(BF16), preferred_element_type=F32)
            b = jnp.dot(h, w3_ref[0, 0].astype(BF16), preferred_element_type=F32)
            t = (_silu(a) * b).astype(BF16)
            prev = jnp.where(j == 0, 0.0, acc_ref[:rows])
            acc_ref[:rows] = prev + jnp.dot(t, w2_ref[0, 0].astype(BF16), preferred_element_type=F32)
            if rows < T_MOE:
                @pl.when(j == 0)
                def _():
                    acc_ref[rows:] = jnp.zeros((T_MOE - rows, D), F32)

    last = j == pl.num_programs(1) - 1

    @pl.when(jnp.logical_and(valid, last))
    def _():
        o_ref[...] = acc_ref[...]

    @pl.when(jnp.logical_and(jnp.logical_not(valid), last))
    def _():
        o_ref[...] = jnp.zeros_like(o_ref)


def _ffn_moe(xs, tile_expert, tile_parts, n_valid, w1, w3, w2, layer):
    nj = D_FF // T_FF

    def jj(i, j, nv):
        return jnp.where(i < nv[0], j, nj - 1)

    grid_spec = pltpu.PrefetchScalarGridSpec(
        num_scalar_prefetch=3,
        grid=(xs.shape[0] // T_MOE, nj),
        in_specs=[
            pl.BlockSpec((T_MOE, D), lambda i, j, te, tp, nv: (i, 0)),
            pl.BlockSpec((1, 1, D, T_FF), lambda i, j, te, tp, nv: (layer, te[i], 0, jj(i, j, nv))),
            pl.BlockSpec((1, 1, D, T_FF), lambda i, j, te, tp, nv: (layer, te[i], 0, jj(i, j, nv))),
            pl.BlockSpec((1, 1, T_FF, D), lambda i, j, te, tp, nv: (layer, te[i], jj(i, j, nv), 0)),
        ],
        out_specs=pl.BlockSpec((T_MOE, D), lambda i, j, te, tp, nv: (i, 0)),
        scratch_shapes=[pltpu.VMEM((T_MOE, D), F32)],
    )
    return pl.pallas_call(
        _ffn_moe_body,
        grid_spec=grid_spec,
        out_shape=jax.ShapeDtypeStruct(xs.shape, F32),
        compiler_params=pltpu.CompilerParams(
            dimension_semantics=("arbitrary", "arbitrary"), vmem_limit_bytes=VMEM_BIG),
        name="ffn_moe",
    )(tile_expert, tile_parts, n_valid, xs, w1, w3, w2)


def _combine_body(start_ref, len_ref, s1_ref, s2_ref, w1_ref, w2_ref, x_ref, gate_ref, os_ref, o_ref,
                  lo_ref, sem):
    i = pl.program_id(0)
    last = pl.num_programs(0) - 1
    slot = i % 2
    other = 1 - slot
    os_at = lambda a, p: os_ref.at[pl.ds(a, p)]

    def gather(tile, buf, start, enable=True):
        plan = [(ln, far, near) for ln, near, far in _tile_plan(tile, len_ref, start_ref)]
        _segment_copies(plan, os_at, lambda a, p: lo_ref.at[buf, pl.ds(a, p)], sem.at[buf], start, enable)

    @pl.when(i == 0)
    def _():
        lo_ref[...] = jnp.zeros_like(lo_ref)
        gather(i, slot, start=True)

    gather(jnp.minimum(i + 1, last), other, start=True, enable=i < last)
    gather(i, slot, start=False)

    lo = lo_ref[slot].astype(BF16)
    lane = lax.broadcasted_iota(I32, (T_ROUTE, S_LOC), 1)
    pick1 = jnp.where(lane == s1_ref[...], 1.0, 0.0).astype(BF16)
    pick2 = jnp.where(lane == s2_ref[...], 1.0, 0.0).astype(BF16)
    y = (w1_ref[...] * jnp.dot(pick1, lo, preferred_element_type=F32)
         + w2_ref[...] * jnp.dot(pick2, lo, preferred_element_type=F32))
    o_ref[...] = x_ref[...] + gate_ref[0] * y


def _combine(x, n, gate, seg_start, seg_len, s1, s2, w1, w2, o_sorted):
    nt = n // T_ROUTE
    col = lambda a: a.reshape(n, 1)
    col_spec = pl.BlockSpec((T_ROUTE, 1), lambda i, *_: (i, 0))
    grid_spec = pltpu.PrefetchScalarGridSpec(
        num_scalar_prefetch=2,
        grid=(nt,),
        in_specs=[col_spec, col_spec, col_spec, col_spec,
                  pl.BlockSpec((T_ROUTE, D), lambda i, *_: (i, 0)),
                  _mod_spec(T_ROUTE),
                  pl.BlockSpec(memory_space=pl.ANY)],
        out_specs=pl.BlockSpec((T_ROUTE, D), lambda i, *_: (i, 0)),
        scratch_shapes=[pltpu.VMEM((2, S_LOC, D), F32), pltpu.SemaphoreType.DMA((2,))],
    )
    return pl.pallas_call(
        _combine_body,
        grid_spec=grid_spec,
        out_shape=jax.ShapeDtypeStruct((n, D), F32),
        compiler_params=pltpu.CompilerParams(
            dimension_semantics=("arbitrary",), vmem_limit_bytes=VMEM_BIG),
        name="moe_combine",
    )(seg_start.reshape(-1), seg_len.reshape(-1), col(s1), col(s2), col(w1), col(w2), x, gate, o_sorted)


def _moe(x, n, sh, sc, gate, ng, router, w1, w3, w2, layer):
    nt = n // T_ROUTE
    s1, s2, p1, p2, pc = _router(x, n, sh, sc, ng, router)
    seg_len = pc[:, :, 0]
    total = jnp.sum(seg_len, axis=0)
    tiles_per = (total + T_MOE - 1) // T_MOE
    tile_ends = jnp.cumsum(tiles_per)
    group_start = (tile_ends - tiles_per) * T_MOE
    seg_start = group_start[None, :] + jnp.cumsum(seg_len, axis=0) - seg_len
    tail_start = group_start + total
    tail_len = tiles_per * T_MOE - total
    n_valid = tile_ends[-1:]
    n_tiles = -(-(2 * n + nt * N_EXPERTS * (SEG_ALIGN - 1)) // T_MOE) + N_EXPERTS
    tile_all = jnp.arange(n_tiles, dtype=I32)
    tile_ids = jnp.minimum(tile_all, n_valid - 1)
    tile_expert = jnp.sum((tile_ids[:, None] >= tile_ends[None, :]).astype(I32), axis=1)
    tile_rows = jnp.clip((group_start + total)[tile_expert] - tile_ids * T_MOE, 0, T_MOE)
    part = T_MOE // MOE_PARTS
    tile_parts = jnp.where(tile_all < n_valid, (tile_rows + part - 1) // part, 0).astype(I32)
    xs = _dispatch(x, n, sh, sc, ng, s1, s2, seg_start, seg_len, tail_start, tail_len, n_valid, n_tiles)
    o_sorted = _ffn_moe(xs, tile_expert, tile_parts, n_valid, w1, w3, w2, layer)
    return _combine(x, n, gate, seg_start, seg_len, s1, s2, p1, p2, o_sorted)


def _swap_halves(y):
    q = HEAD_DIM // 4
    return jnp.concatenate([y[q:2 * q], y[:q], y[3 * q:], y[2 * q:3 * q]], axis=0)


def _norm_rope_t(zh, c_tab, s_tab):
    ss = jnp.mean(zh * zh, axis=0, keepdims=True)
    yn = zh * lax.rsqrt(ss + EPS)
    return yn * c_tab + _swap_halves(yn) * s_tab


def _qkv_body(x_ref, sh_ref, sc_ref, ng_ref, w_ref, cq_ref, sq_ref, ck_ref, sk_ref, q_ref, k_ref, v_ref):
    h = _modnorm(x_ref[...], ng_ref[...], sc_ref[0], sh_ref[0]).astype(BF16)
    zt = lax.dot_general(w_ref[...], h, (((1,), (1,)), ((), ())), preferred_element_type=F32)
    nq = N_HEADS * HEAD_DIM
    nkv = N_KV * HEAD_DIM
    cq, sq, ck, sk = cq_ref[...], sq_ref[...], ck_ref[...], sk_ref[...]
    for hd in range(N_HEADS):
        lo = hd * HEAD_DIM
        q_ref[lo:lo + HEAD_DIM, :] = _norm_rope_t(zt[lo:lo + HEAD_DIM], cq, sq).astype(BF16)
    kt = jnp.concatenate(
        [_norm_rope_t(zt[nq + kh * HEAD_DIM:nq + (kh + 1) * HEAD_DIM], ck, sk) for kh in range(N_KV)],
        axis=0)
    k_ref[...] = kt.T.astype(BF16)
    v_ref[...] = zt[nq + nkv:].astype(BF16)


def _rope_tables(q_g, k_g):
    half = HEAD_DIM // 2
    quarter = HEAD_DIM // 4
    pos = jnp.arange(SEQ)
    pos_row = (pos // GRID_W).astype(F32)
    pos_col = (pos % GRID_W).astype(F32)
    inv_freq = ROPE_THETA ** (-jnp.arange(0, half, 2, dtype=F32) / half)
    ang_row = inv_freq[:, None] * pos_row[None, :]
    ang_col = inv_freq[:, None] * pos_col[None, :]
    ang = jnp.concatenate([ang_row, ang_row, ang_col, ang_col], axis=0)
    cos = jnp.concatenate([jnp.cos(ang), jnp.ones((HEAD_DIM, T_QKV), F32)], axis=1)
    sin = jnp.concatenate([jnp.sin(ang), jnp.zeros((HEAD_DIM, T_QKV), F32)], axis=1)
    first = ((jnp.arange(HEAD_DIM) % half) < quarter)[:, None]
    sin = jnp.where(first, -sin, sin)

    def tables(g, scale):
        g = g.astype(F32) * scale
        partner = _swap_halves(g[:, None])
        return g[:, None] * cos, partner * sin

    return tables(q_g, HEAD_DIM ** -0.5 * LOG2_E) + tables(k_g, 1.0)


def _qkv(x, sh, sc, ng, w_qkv, q_g, k_g):
    n = x.shape[0]
    nt = n // T_QKV
    nkv = N_KV * HEAD_DIM
    wd = w_qkv.shape[1]
    per_seq = SEQ // T_QKV
    tab_spec = pl.BlockSpec(
        (HEAD_DIM, T_QKV), lambda i: (0, jnp.where(i < N_LAT // T_QKV, i % per_seq, per_seq)))
    mod = _mod_spec(T_QKV)
    return pl.pallas_call(
        _qkv_body,
        grid=(nt,),
        in_specs=[
            pl.BlockSpec((T_QKV, D), lambda i: (i, 0)),
            mod, mod,
            _const_spec((1, D)),
            _const_spec((wd, D)),
            tab_spec, tab_spec, tab_spec, tab_spec,
        ],
        out_specs=[pl.BlockSpec((D, T_QKV), lambda i: (0, i)),
                   pl.BlockSpec((T_QKV, nkv), lambda i: (i, 0)),
                   pl.BlockSpec((nkv, T_QKV), lambda i: (0, i))],
        out_shape=[jax.ShapeDtypeStruct((D, n), BF16),
                   jax.ShapeDtypeStruct((n, nkv), BF16),
                   jax.ShapeDtypeStruct((nkv, n), BF16)],
        compiler_params=pltpu.CompilerParams(
            dimension_semantics=("arbitrary",), vmem_limit_bytes=VMEM_BIG),
        name="attn_qkv",
    )(x, sh, sc, ng.reshape(1, D), w_qkv.T.astype(BF16), *_rope_tables(q_g, k_g))


def _attn_heads(qt_ref, k_all, vt_all, band, sink_ref, o_ref):
    group = N_HEADS // N_KV
    nk = k_all.shape[0]
    ones = jnp.ones((BF16_ROWS, nk), BF16)
    zeros = jnp.zeros((HEAD_DIM, group * T_Q), BF16)
    outs = []
    scores = []
    probs = []
    for kh in range(N_KV):
        q4 = jnp.concatenate(
            [qt_ref[(kh * group + g) * HEAD_DIM:(kh * group + g + 1) * HEAD_DIM, :] for g in range(group)],
            axis=1)
        qpad = jnp.concatenate([q4, zeros] if kh % 2 == 0 else [zeros, q4], axis=0)
        k2 = k_all[:, (kh // 2) * LANES:(kh // 2 + 1) * LANES]
        scores.append(jnp.dot(k2, qpad, preferred_element_type=F32))
    for kh in range(N_KV):
        st = scores[kh]
        blocks = [st[c * T_Q:(c + 1) * T_Q] for c in range(nk // T_Q)]
        if band is not None:
            blocks[0] = jnp.where(band[0], blocks[0], NEG_INF)
            blocks[2] = jnp.where(band[1], blocks[2], NEG_INF)
        best = blocks[0]
        for blk in blocks[1:]:
            best = jnp.maximum(best, blk)
        sink = jnp.concatenate(
            [jnp.full((1, T_Q), sink_ref[kh * group + g] * LOG2_E, F32) for g in range(group)], axis=1)
        m = jnp.maximum(jnp.max(best, axis=0, keepdims=True), sink)
        pt = jnp.concatenate([jnp.exp2(blk - m).astype(BF16) for blk in blocks], axis=0)
        probs.append((pt, jnp.exp2(sink - m)))
    for kh in range(N_KV):
        pt, sink_term = probs[kh]
        vt_aug = jnp.concatenate([vt_all[kh * HEAD_DIM:(kh + 1) * HEAD_DIM, :], ones], axis=0)
        ot = jnp.dot(vt_aug, pt, preferred_element_type=F32)
        den = ot[HEAD_DIM:HEAD_DIM + 1] + sink_term
        o = ot[:HEAD_DIM] / den
        outs.extend(o[:, g * T_Q:(g + 1) * T_Q] for g in range(group))
    o_ref[...] = jnp.concatenate(outs, axis=0).T.astype(BF16)


def _attn_body(sink_ref, q_ref, kp_ref, kc_ref, kn_ref, kx_ref, vp_ref, vc_ref, vn_ref, vx_ref, o_ref,
               *, n_q_blocks):
    iq = pl.program_id(1)
    per_seq = SEQ // T_Q
    group = N_HEADS // N_KV

    @pl.when(iq < per_seq)
    def _():
        k_all = jnp.concatenate([kp_ref[...], kc_ref[...], kn_ref[...], kx_ref[...]], axis=0)
        vt_all = jnp.concatenate([vp_ref[...], vc_ref[...], vn_ref[...], vx_ref[...]], axis=1)
        c = lax.broadcasted_iota(I32, (T_Q, group * T_Q), 0)
        r = lax.broadcasted_iota(I32, (T_Q, group * T_Q), 1) % T_Q
        far = 2 * T_Q
        mask_prev = c >= r + jnp.where(iq > 0, 0, far)
        mask_next = c <= r - jnp.where(iq < per_seq - 1, 0, far)
        _attn_heads(q_ref, k_all, vt_all, (mask_prev, mask_next), sink_ref, o_ref)

    if n_q_blocks > per_seq:
        @pl.when(iq >= per_seq)
        def _():
            _attn_heads(q_ref, kx_ref[...], vx_ref[...], None, sink_ref, o_ref)


def _attention(qt, k, vt, sink, need_ctx):
    assert T_Q == WINDOW
    per_seq = SEQ // T_Q
    ctx_blocks = CTX_LEN // T_Q
    n_q_blocks = per_seq + (ctx_blocks if need_ctx else 0)
    n_out = N_ALL if need_ctx else N_LAT
    lat_blocks = N_LAT // T_Q
    kw = k.shape[1]

    def q_blk(b, iq):
        return jnp.where(iq < per_seq, b * per_seq + iq, lat_blocks + b * ctx_blocks + (iq - per_seq))

    def win_blk(b, iq, off):
        return b * per_seq + jnp.clip(iq + off, 0, per_seq - 1)

    ctx_blk = lambda b: N_LAT // CTX_LEN + b
    k_win = lambda off: pl.BlockSpec((T_Q, kw), lambda b, iq, *_: (win_blk(b, iq, off), 0))
    v_win = lambda off: pl.BlockSpec((kw, T_Q), lambda b, iq, *_: (0, win_blk(b, iq, off)))
    k_ctx = pl.BlockSpec((CTX_LEN, kw), lambda b, iq, *_: (ctx_blk(b), 0))
    v_ctx = pl.BlockSpec((kw, CTX_LEN), lambda b, iq, *_: (0, ctx_blk(b)))
    grid_spec = pltpu.PrefetchScalarGridSpec(
        num_scalar_prefetch=1,
        grid=(NB, n_q_blocks),
        in_specs=[pl.BlockSpec((D, T_Q), lambda b, iq, *_: (0, q_blk(b, iq))),
                  k_win(-1), k_win(0), k_win(1), k_ctx, v_win(-1), v_win(0), v_win(1), v_ctx],
        out_specs=pl.BlockSpec((T_Q, D), lambda b, iq, *_: (q_blk(b, iq), 0)),
    )
    return pl.pallas_call(
        functools.partial(_attn_body, n_q_blocks=n_q_blocks),
        grid_spec=grid_spec,
        out_shape=jax.ShapeDtypeStruct((n_out, D), BF16),
        compiler_params=pltpu.CompilerParams(
            dimension_semantics=("arbitrary", "arbitrary"), vmem_limit_bytes=VMEM_BIG),
        name="attn_core",
    )(sink.astype(F32), qt, k, k, k, k, vt, vt, vt, vt)


def _proj_body(a_ref, x_ref, gate_ref, w_ref, o_ref):
    y = jnp.dot(a_ref[...], w_ref[...], preferred_element_type=F32)
    o_ref[...] = x_ref[...] + gate_ref[0] * y


def _proj_residual(a, x, gate, w):
    n = a.shape[0]
    return pl.pallas_call(
        _proj_body,
        grid=(n // T_PROJ,),
        in_specs=[pl.BlockSpec((T_PROJ, D), lambda i: (i, 0)),
                  pl.BlockSpec((T_PROJ, D), lambda i: (i, 0)),
                  _mod_spec(T_PROJ),
                  _const_spec((D, D))],
        out_specs=pl.BlockSpec((T_PROJ, D), lambda i: (i, 0)),
        out_shape=jax.ShapeDtypeStruct((n, D), F32),
        compiler_params=pltpu.CompilerParams(dimension_semantics=("arbitrary",)),
        name="attn_proj",
    )(a, x, gate, w.astype(BF16))


def kernel(x, c, ctx, c_ctx, ada_w, ada_b, norm_mix_g, norm_ffn_g, ev_w_in, ev_ln_g, ev_ln_b, ev_ws,
           ev_bs, ev_conv_w, ev_conv_b, ev_cnorm_g, ev_w_out, od_w_qkv, od_q_g, od_k_g, od_sink, od_w_o,
           ff_w1, ff_w3, ff_w2, moe_router, moe_w1, moe_w3, moe_w2):
    assert x.shape == (NB, SEQ, D) and ctx.shape == (NB, CTX_LEN, D)
    mods = _ada_mods(c, c_ctx, ada_w, ada_b)
    xa = None
    ev_w_in_b, ev_w_out_b = ev_w_in.astype(BF16), ev_w_out.astype(BF16)
    for li in range(DEPTH):
        need_ctx = li < DEPTH - 1
        j = li // 2
        sh1, sc1, g1, sh2, sc2, g2 = mods[li]
        if li % 2 == 0:
            xin = (x.reshape(N_LAT, D), ctx.reshape(N_CTX, D)) if li == 0 else (xa, None)
            xa = _even_mixer(*xin, sh1, sc1, g1, norm_mix_g[li], ev_w_in_b, ev_ln_g[j], ev_ln_b[j],
                             ev_ws[j], ev_bs[j], ev_conv_w[j], ev_conv_b[j], ev_cnorm_g[j], ev_w_out_b, j)
            xa = _ffn_dense(xa, sh2, sc2, g2, norm_ffn_g[li], ff_w1, ff_w3, ff_w2, j)
        else:
            qt, k, vt = _qkv(xa, sh1, sc1, norm_mix_g[li], od_w_qkv[j], od_q_g[j], od_k_g[j])
            o = _attention(qt, k, vt, od_sink[j], need_ctx)
            xa = _proj_residual(o, xa, g1, od_w_o[j])
            n = N_ALL if need_ctx else N_LAT
            xa = _moe(xa, n, sh2, sc2, g2, norm_ffn_g[li], moe_router[j], moe_w1, moe_w3, moe_w2, j)
    return xa[:N_LAT].reshape(NB, SEQ, D)
```

```python
import functools

import jax
import jax.numpy as jnp
from jax import lax
from jax.experimental import pallas as pl
from jax.experimental.pallas import tpu as pltpu

F32 = jnp.float32
BF16 = jnp.bfloat16
I32 = jnp.int32

D = 1024
NB = 8
SEQ = 2048
CTX_LEN = 256
DEPTH = 4
GRID_W = 64
CHUNK = 128
A_GROUPS = 8
CONV_W = 31
N_HEADS = 16
N_KV = 4
HEAD_DIM = 64
WINDOW = 128
ROPE_THETA = 10000.0
D_FF = 3584
N_EXPERTS = 8
EPS = 1e-6
NEG_INF = -1e30
LOG2_E = 1.4426950408889634

N_LAT = NB * SEQ
N_CTX = NB * CTX_LEN
N_ALL = N_LAT + N_CTX
CTX_MOD_ROW = NB

LANES = 128
SUBLANES = 8
BF16_ROWS = 16

T_EVEN = 256
EVEN_PAIR = 2
HALO = BF16_ROWS
T_FFN = 1024
T_FF = 512
T_ROUTE = 512
T_MOE = 1024
MOE_PARTS = 4
SEG_ALIGN = SUBLANES
S_LOC = 2 * T_ROUTE + N_EXPERTS * SEG_ALIGN
SEG_SIZES = tuple(T_ROUTE >> s for s in range(T_ROUTE.bit_length()) if T_ROUTE >> s >= SEG_ALIGN)
assert SEG_SIZES[-1] == SEG_ALIGN
T_QKV = 256
T_Q = 128

VMEM_BIG = 52 * 1024 * 1024


def _mod_spec(tile):
    per = SEQ // tile
    return pl.BlockSpec((1, 1, D), lambda i, *_: (jnp.minimum(i // per, CTX_MOD_ROW), 0, 0))


def _const_spec(shape):
    nd = len(shape)
    return pl.BlockSpec(shape, lambda *_: (0,) * nd)


def _modnorm(x, g, sc, sh):
    ms = jnp.mean(x * x, axis=-1, keepdims=True)
    return (x * lax.rsqrt(ms + EPS) * g) * (1.0 + sc) + sh


def _silu(x):
    return x * jax.nn.sigmoid(x)


def _ada_body(c_ref, w_ref, b_ref, o_ref):
    a = _silu(c_ref[...])
    o_ref[0] = jnp.dot(a.astype(BF16), w_ref[0].astype(BF16), preferred_element_type=F32) + b_ref[0]


def _ada_mods(c, c_ctx, ada_w, ada_b):
    rows = BF16_ROWS
    cc = jnp.concatenate([c, c_ctx[None, :], jnp.zeros((rows - NB - 1, D), F32)], axis=0)
    out = pl.pallas_call(
        _ada_body,
        grid=(DEPTH, 6),
        in_specs=[
            pl.BlockSpec((rows, D), lambda l, n: (0, 0)),
            pl.BlockSpec((1, D, D), lambda l, n: (l, 0, n)),
            pl.BlockSpec((1, 1, D), lambda l, n: (l, 0, n)),
        ],
        out_specs=pl.BlockSpec((1, rows, D), lambda l, n: (l, 0, n)),
        out_shape=jax.ShapeDtypeStruct((DEPTH, rows, 6 * D), F32),
        name="ada_mod",
    )(cc, ada_w, ada_b.reshape(DEPTH, 1, 6 * D))
    m = out[:, :NB + 1].reshape(DEPTH, NB + 1, 6, 1, D)
    return [[m[l, :, k] for k in range(6)] for l in range(DEPTH)]


def _even_body(x_ref, xc_ref, xp_ref, xn_ref, sh_ref, sc_ref, gate_ref, ng_ref, win_ref, lng_ref, lnb_ref,
               ws_ref, bs_ref, cw_ref, cb_ref, cng_ref, wout_ref, o_ref, gext_ref, *, split):
    step = pl.program_id(0)
    per_seq = SEQ // T_EVEN
    ext = T_EVEN + 2 * HALO
    halves = range(EVEN_PAIR)
    x_all = x_ref[...]
    if split:
        x_all = jnp.where(step < N_LAT // (EVEN_PAIR * T_EVEN), x_all, xc_ref[...])
    tiles = [x_all[hf * T_EVEN:(hf + 1) * T_EVEN] for hf in halves]
    befores = [xp_ref[...]] + [tiles[hf - 1][T_EVEN - HALO:] for hf in halves[1:]]
    afters = [tiles[hf + 1][:HALO] for hf in halves[:-1]] + [xn_ref[...]]

    hs = [_modnorm(jnp.concatenate([befores[hf], tiles[hf], afters[hf]], axis=0),
                   ng_ref[...], sc_ref[0], sh_ref[0]).astype(BF16) for hf in halves]
    zbs = [jnp.dot(hs[hf], win_ref[0, :, 2 * D:], preferred_element_type=F32) for hf in halves]
    zas = [jnp.dot(hs[hf][HALO:HALO + T_EVEN], win_ref[0, :, :2 * D], preferred_element_type=F32)
           for hf in halves]

    for hf in halves:
        i = step * EVEN_PAIR + hf
        is_lat = i < N_LAT // T_EVEN
        pos = i % per_seq
        is_start = jnp.logical_or(jnp.logical_not(is_lat), pos == 0)
        is_end = jnp.logical_or(jnp.logical_not(is_lat), pos == per_seq - 1)
        gg = zbs[hf][:, :D] * jax.nn.sigmoid(zbs[hf][:, D:])
        row = lax.broadcasted_iota(I32, (ext, 1), 0)
        lo = jnp.where(is_start, HALO, 0)
        hi = jnp.where(is_end, T_EVEN + HALO, ext)
        gg = jnp.where(jnp.logical_and(row >= lo, row < hi), gg, 0.0)
        for cbk in range(D // LANES):
            gext_ref[hf, cbk] = gg[:, cbk * LANES:(cbk + 1) * LANES]

    rows_per = 64
    gdim = D // A_GROUPS
    for hf in halves:
        cols = []
        for cbk in range(D // LANES):
            blocks = []
            for rb in range(T_EVEN // rows_per):
                acc = jnp.zeros((rows_per, LANES), F32)
                for k in range(CONV_W):
                    off = rb * rows_per + k + HALO - CONV_W // 2
                    acc = acc + cw_ref[cbk, pl.ds(k, 1), :] * gext_ref[hf, cbk, pl.ds(off, rows_per), :]
                blocks.append(acc)
            cols.append(jnp.concatenate(blocks, axis=0))

        cv = jnp.concatenate(cols, axis=1) + cb_ref[...]
        ms = jnp.mean(cv * cv, axis=-1, keepdims=True)
        yb = _silu(cv * lax.rsqrt(ms + EPS) * cng_ref[...])

        u = jax.nn.gelu(zas[hf][:, :D])
        v = jax.nn.gelu(zas[hf][:, D:])
        mu = jnp.mean(v, axis=-1, keepdims=True)
        vc = v - mu
        var = jnp.mean(vc * vc, axis=-1, keepdims=True)
        vn = (vc * lax.rsqrt(var + EPS) * lng_ref[...] + lnb_ref[...]).astype(BF16)
        chunks = []
        for ck in range(T_EVEN // CHUNK):
            blocks = [
                jnp.dot(ws_ref[g], vn[ck * CHUNK:(ck + 1) * CHUNK, g * gdim:(g + 1) * gdim],
                        preferred_element_type=F32)
                for g in range(A_GROUPS)
            ]
            chunks.append(jnp.concatenate(blocks, axis=1) + bs_ref[...])
        ya = u * jnp.concatenate(chunks, axis=0)

        y = (jnp.dot(ya.astype(BF16), wout_ref[0, :D], preferred_element_type=F32)
             + jnp.dot(yb.astype(BF16), wout_ref[0, D:], preferred_element_type=F32))
        o_ref[hf * T_EVEN:(hf + 1) * T_EVEN, :] = tiles[hf] + gate_ref[0] * y


def _even_mixer(x, x_ctx, sh, sc, gate, ng, w_in, ln_g, ln_b, ws, bs, conv_w, conv_b, cn_g, w_out, j):
    split = x_ctx is not None
    n = x.shape[0] + (x_ctx.shape[0] if split else 0)
    rows = EVEN_PAIR * T_EVEN
    nt = n // rows
    hb = rows // HALO
    last = x.shape[0] // HALO - 1
    x_steps = x.shape[0] // rows
    if split:
        main_specs = [pl.BlockSpec((rows, D), lambda i: (jnp.minimum(i, x_steps - 1), 0)),
                      pl.BlockSpec((rows, D), lambda i: (jnp.maximum(i - x_steps, 0), 0))]
        x_second = x_ctx
    else:
        main_specs = [pl.BlockSpec((rows, D), lambda i: (i, 0)), _const_spec((HALO, D))]
        x_second = x[:HALO]
    ncb = D // LANES
    bs_full = jnp.repeat(bs.T, D // A_GROUPS, axis=1)
    taps = -(-CONV_W // SUBLANES) * SUBLANES
    cw = jnp.pad(conv_w, ((0, taps - CONV_W), (0, 0))).reshape(taps, ncb, LANES).transpose(1, 0, 2)
    mod = _mod_spec(rows)
    row1 = lambda a: a.reshape(1, D)
    return pl.pallas_call(
        functools.partial(_even_body, split=split),
        grid=(nt,),
        in_specs=main_specs + [
            pl.BlockSpec((HALO, D), lambda i: (jnp.clip(i * hb - 1, 0, last), 0)),
            pl.BlockSpec((HALO, D), lambda i: (jnp.minimum((i + 1) * hb, last), 0)),
            mod, mod, mod,
            _const_spec((1, D)),
            pl.BlockSpec((1, D, 4 * D), lambda i: (j, 0, 0)),
            _const_spec((1, D)), _const_spec((1, D)),
            _const_spec((A_GROUPS, CHUNK, CHUNK)),
            _const_spec((CHUNK, D)),
            _const_spec((ncb, taps, LANES)),
            _const_spec((1, D)), _const_spec((1, D)),
            pl.BlockSpec((1, 2 * D, D), lambda i: (j, 0, 0)),
        ],
        out_specs=pl.BlockSpec((rows, D), lambda i: (i, 0)),
        out_shape=jax.ShapeDtypeStruct((n, D), F32),
        scratch_shapes=[
            pltpu.VMEM((EVEN_PAIR, ncb, T_EVEN + 2 * HALO, LANES), F32),
        ],
        compiler_params=pltpu.CompilerParams(
            dimension_semantics=("arbitrary",), vmem_limit_bytes=VMEM_BIG),
        name="even_mixer",
    )(x, x_second, x, x, sh, sc, gate, row1(ng), w_in, row1(ln_g), row1(ln_b), ws.astype(BF16),
      bs_full, cw, row1(conv_b), row1(cn_g), w_out)


def _ffn_dense_body(x_ref, sh_ref, sc_ref, gate_ref, ng_ref, w1_ref, w3_ref, w2_ref, o_ref,
                    h_ref, acc_ref):
    j = pl.program_id(1)

    @pl.when(jnp.logical_and(pl.program_id(0) == 0, j == 0))
    def _():
        acc_ref[...] = jnp.zeros_like(acc_ref)

    @pl.when(j == 0)
    def _():
        h_ref[...] = _modnorm(x_ref[...], ng_ref[...], sc_ref[0], sh_ref[0]).astype(BF16)

    h = h_ref[...]
    a = jnp.dot(h, w1_ref[0].astype(BF16), preferred_element_type=F32)
    b = jnp.dot(h, w3_ref[0].astype(BF16), preferred_element_type=F32)
    t = (_silu(a) * b).astype(BF16)
    prev = jnp.where(j == 0, 0.0, acc_ref[...])
    acc_ref[...] = prev + jnp.dot(t, w2_ref[0].astype(BF16), preferred_element_type=F32)

    @pl.when(j == pl.num_programs(1) - 1)
    def _():
        o_ref[...] = x_ref[...] + gate_ref[0] * acc_ref[...]


def _ffn_dense(x, sh, sc, gate, ng, w1, w3, w2, layer):
    n = x.shape[0]
    mod = _mod_spec(T_FFN)
    return pl.pallas_call(
        _ffn_dense_body,
        grid=(n // T_FFN, D_FF // T_FF),
        in_specs=[
            pl.BlockSpec((T_FFN, D), lambda i, j: (i, 0)),
            mod, mod, mod,
            _const_spec((1, D)),
            pl.BlockSpec((1, D, T_FF), lambda i, j: (layer, 0, j)),
            pl.BlockSpec((1, D, T_FF), lambda i, j: (layer, 0, j)),
            pl.BlockSpec((1, T_FF, D), lambda i, j: (layer, j, 0)),
        ],
        out_specs=pl.BlockSpec((T_FFN, D), lambda i, j: (i, 0)),
        out_shape=jax.ShapeDtypeStruct((n, D), F32),
        scratch_shapes=[pltpu.VMEM((T_FFN, D), BF16), pltpu.VMEM((T_FFN, D), F32)],
        compiler_params=pltpu.CompilerParams(
            dimension_semantics=("arbitrary", "arbitrary"), vmem_limit_bytes=VMEM_BIG),
        name="ffn_dense",
    )(x, sh, sc, gate, ng.reshape(1, D), w1, w3, w2)


def _router_body(a_ref, wo_ref, g1_ref, x_ref, sh_ref, sc_ref, ng_ref, rt_ref,
                 xo_ref, hb_ref, s1_ref, s2_ref, w1_ref, w2_ref, pc_ref):
    xn = x_ref[...] + g1_ref[0] * jnp.dot(a_ref[...], wo_ref[...], preferred_element_type=F32)
    xo_ref[...] = xn
    h = _modnorm(xn, ng_ref[...], sc_ref[0], sh_ref[0])
    hb_ref[...] = h.astype(BF16)
    lg = lax.dot_general(rt_ref[...], h, (((1,), (1,)), ((), ())),
                         precision=lax.Precision.HIGHEST, preferred_element_type=F32)
    e = lax.broadcasted_iota(I32, lg.shape, 0).astype(F32)
    big = float(N_EXPERTS)
    m1 = jnp.max(lg, axis=0, keepdims=True)
    i1 = jnp.min(jnp.where(lg == m1, e, big), axis=0, keepdims=True)
    lg2 = jnp.where(e == i1, -jnp.inf, lg)
    m2 = jnp.max(lg2, axis=0, keepdims=True)
    i2 = jnp.min(jnp.where(lg2 == m2, e, big), axis=0, keepdims=True)
    e2 = jnp.exp(m2 - m1)
    den = 1.0 + e2
    w1_ref[0] = 1.0 / den
    w2_ref[0] = e2 / den

    sel1 = e == i1
    sel2 = e == i2
    member = jnp.where(jnp.logical_or(sel1, sel2), 1.0, 0.0)
    t = lg.shape[1]
    before = (lax.broadcasted_iota(I32, (t, t), 0) < lax.broadcasted_iota(I32, (t, t), 1))
    tri = jnp.where(before, 1.0, 0.0).astype(BF16)
    rank = jnp.dot(member.astype(BF16), tri, preferred_element_type=F32)
    count = jnp.sum(member, axis=1, keepdims=True)
    padded = jnp.ceil(count * (1.0 / SEG_ALIGN)) * SEG_ALIGN
    e_col = lax.broadcasted_iota(I32, (N_EXPERTS, 1), 0)
    seg_start = jnp.zeros((N_EXPERTS, 1), F32)
    for ex in range(1, N_EXPERTS):
        below = jnp.sum(jnp.where(e_col < ex, padded, 0.0), axis=0, keepdims=True)
        seg_start = jnp.where(e_col == ex, below, seg_start)
    slot = rank + seg_start
    s1 = jnp.sum(jnp.where(sel1, slot, 0.0), axis=0, keepdims=True).astype(I32)
    s2 = jnp.sum(jnp.where(sel2, slot, 0.0), axis=0, keepdims=True).astype(I32)
    s1_ref[0] = s1
    s2_ref[0] = s2
    pc_ref[0] = jnp.broadcast_to(padded.astype(I32), (N_EXPERTS, LANES))


def _router(a, w_o, gate1, x, sh, sc, ng, router):
    n = a.shape[0]
    nt = n // T_ROUTE
    mod = _mod_spec(T_ROUTE)
    row_spec = pl.BlockSpec((T_ROUTE, D), lambda i: (i, 0))
    vec_spec = pl.BlockSpec((1, 1, T_ROUTE), lambda i: (i, 0, 0))
    vec_i = jax.ShapeDtypeStruct((nt, 1, T_ROUTE), I32)
    vec_f = jax.ShapeDtypeStruct((nt, 1, T_ROUTE), F32)
    return pl.pallas_call(
        _router_body,
        grid=(nt,),
        in_specs=[
            row_spec,
            _const_spec((D, D)),
            mod,
            row_spec,
            mod, mod,
            _const_spec((1, D)),
            _const_spec((N_EXPERTS, D)),
        ],
        out_specs=[row_spec, row_spec, vec_spec, vec_spec, vec_spec, vec_spec,
                   pl.BlockSpec((1, N_EXPERTS, LANES), lambda i: (i, 0, 0))],
        out_shape=[jax.ShapeDtypeStruct((n, D), F32), jax.ShapeDtypeStruct((n, D), BF16),
                   vec_i, vec_i, vec_f, vec_f, jax.ShapeDtypeStruct((nt, N_EXPERTS, LANES), I32)],
        compiler_params=pltpu.CompilerParams(
            dimension_semantics=("arbitrary",), vmem_limit_bytes=VMEM_BIG),
        name="moe_router",
    )(a, w_o.astype(BF16), gate1, x, sh, sc, ng.reshape(1, D), router.T)


def _dispatch_body(start_ref, len_ref, tail_start_ref, tail_len_ref, nv_ref,
                   h_ref, s1_ref, s2_ref, xs_ref, lbuf_ref, zbuf_ref, sem,
                   *, n_tiles, min_tiles):
    i = pl.program_id(0)
    slot = i % 2
    other = 1 - slot
    h = h_ref[...]
    rows = lax.broadcasted_iota(I32, (S_LOC, T_ROUTE), 0)
    perm = jnp.where(jnp.logical_or(rows == s1_ref[0], rows == s2_ref[0]), 1.0, 0.0).astype(BF16)
    lbuf_ref[slot] = jnp.dot(perm, h, preferred_element_type=F32)

    xs_at = lambda a, p: xs_ref.at[pl.ds(a, p)]
    _segment_copies(_tile_plan(i, len_ref, start_ref), lambda a, p: lbuf_ref.at[slot, pl.ds(a, p)], xs_at,
                    sem.at[slot], start=True)
    _segment_copies(_tile_plan(jnp.maximum(i - 1, 0), len_ref, start_ref),
                    lambda a, p: lbuf_ref.at[other, pl.ds(a, p)], xs_at, sem.at[other], start=False,
                    enable=i > 0)

    @pl.when(i == pl.num_programs(0) - 1)
    def _():
        _segment_copies(_tile_plan(i, len_ref, start_ref), lambda a, p: lbuf_ref.at[slot, pl.ds(a, p)],
                        xs_at, sem.at[slot], start=False)
        zbuf_ref[...] = jnp.zeros_like(zbuf_ref)
        tails = [(tail_len_ref[ex], 0, tail_start_ref[ex]) for ex in range(N_EXPERTS)]
        zero_at = lambda a, p: zbuf_ref.at[pl.ds(a, p)]
        _segment_copies(tails, zero_at, xs_at, sem.at[slot], start=True)
        _segment_copies(tails, zero_at, xs_at, sem.at[slot], start=False)
        for tile in range(min_tiles, n_tiles):
            @pl.when(tile >= nv_ref[0])
            def _(tile=tile):
                cp = pltpu.make_async_copy(zbuf_ref, xs_ref.at[pl.ds(tile * T_MOE, T_MOE)], sem.at[slot])
                cp.start()
                cp.wait()


def _dispatch(h, s1, s2, seg_start, seg_len, tail_start, tail_len, n_valid, n_tiles):
    n = h.shape[0]
    nt = n // T_ROUTE
    vec_spec = pl.BlockSpec((1, 1, T_ROUTE), lambda i, *_: (i, 0, 0))
    grid_spec = pltpu.PrefetchScalarGridSpec(
        num_scalar_prefetch=5,
        grid=(nt,),
        in_specs=[pl.BlockSpec((T_ROUTE, D), lambda i, *_: (i, 0)),
                  vec_spec, vec_spec],
        out_specs=pl.BlockSpec(memory_space=pl.ANY),
        scratch_shapes=[pltpu.VMEM((2, S_LOC, D), F32), pltpu.VMEM((T_MOE, D), F32),
                        pltpu.SemaphoreType.DMA((2,))],
    )
    return pl.pallas_call(
        functools.partial(_dispatch_body, n_tiles=n_tiles, min_tiles=(2 * n) // T_MOE),
        grid_spec=grid_spec,
        out_shape=jax.ShapeDtypeStruct((n_tiles * T_MOE, D), F32),
        compiler_params=pltpu.CompilerParams(
            dimension_semantics=("arbitrary",), vmem_limit_bytes=VMEM_BIG),
        name="moe_dispatch",
    )(seg_start.reshape(-1), seg_len.reshape(-1), tail_start, tail_len, n_valid, h, s1, s2)


def _tile_plan(tile, len_ref, far_ref):
    near = 0
    plan = []
    for ex in range(N_EXPERTS):
        ln = len_ref[tile * N_EXPERTS + ex]
        plan.append((ln, near, far_ref[tile * N_EXPERTS + ex]))
        near = near + ln
    return plan


def _segment_copies(plan, src_at, dst_at, sem, start, enable=True):
    for ln, src, dst in plan:
        for p in SEG_SIZES:
            off = jnp.bitwise_and(ln, -(2 * p))

            @pl.when(jnp.logical_and(jnp.bitwise_and(ln, p) != 0, enable))
            def _(off=off, src=src, dst=dst, p=p):
                cp = pltpu.make_async_copy(src_at(pl.multiple_of(src + off, SEG_ALIGN), p),
                                           dst_at(pl.multiple_of(dst + off, SEG_ALIGN), p), sem)
                if start:
                    cp.start()
                else:
                    cp.wait()


def _ffn_moe_body(te_ref, parts_ref, nv_ref, xs_ref, w1_ref, w3_ref, w2_ref, o_ref, acc_ref):
    del te_ref
    i = pl.program_id(0)
    j = pl.program_id(1)
    valid = i < nv_ref[0]

    @pl.when(jnp.logical_and(i == 0, j == 0))
    def _():
        acc_ref[...] = jnp.zeros_like(acc_ref)

    for parts in range(1, MOE_PARTS + 1):
        @pl.when(parts_ref[i] == parts)
        def _(rows=parts * (T_MOE // MOE_PARTS)):
            h = xs_ref[:rows].astype(BF16)
            a = jnp.dot(h, w1_ref[0, 0].astype(BF16), preferred_element_type=F32)
            b = jnp.dot(h, w3_ref[0, 0].astype(BF16), preferred_element_type=F32)
            t = (_silu(a) * b).astype(BF16)
            prev = jnp.where(j == 0, 0.0, acc_ref[:rows])
            acc_ref[:rows] = prev + jnp.dot(t, w2_ref[0, 0].astype(BF16), preferred_element_type=F32)
            if rows < T_MOE:
                @pl.when(j == 0)
                def _():
                    acc_ref[rows:] = jnp.zeros((T_MOE - rows, D), F32)

    last = j == pl.num_programs(1) - 1

    @pl.when(jnp.logical_and(valid, last))
    def _():
        o_ref[...] = acc_ref[...]

    @pl.when(jnp.logical_and(jnp.logical_not(valid), last))
    def _():
        o_ref[...] = jnp.zeros_like(o_ref)


def _ffn_moe(xs, tile_expert, tile_parts, n_valid, w1, w3, w2, layer):
    nj = D_FF // T_FF

    def jj(i, j, nv):
        return jnp.where(i < nv[0], j, nj - 1)

    grid_spec = pltpu.PrefetchScalarGridSpec(
        num_scalar_prefetch=3,
        grid=(xs.shape[0] // T_MOE, nj),
        in_specs=[
            pl.BlockSpec((T_MOE, D), lambda i, j, te, tp, nv: (i, 0)),
            pl.BlockSpec((1, 1, D, T_FF), lambda i, j, te, tp, nv: (layer, te[i], 0, jj(i, j, nv))),
            pl.BlockSpec((1, 1, D, T_FF), lambda i, j, te, tp, nv: (layer, te[i], 0, jj(i, j, nv))),
            pl.BlockSpec((1, 1, T_FF, D), lambda i, j, te, tp, nv: (layer, te[i], jj(i, j, nv), 0)),
        ],
        out_specs=pl.BlockSpec((T_MOE, D), lambda i, j, te, tp, nv: (i, 0)),
        scratch_shapes=[pltpu.VMEM((T_MOE, D), F32)],
    )
    return pl.pallas_call(
        _ffn_moe_body,
        grid_spec=grid_spec,
        out_shape=jax.ShapeDtypeStruct(xs.shape, F32),
        compiler_params=pltpu.CompilerParams(
            dimension_semantics=("arbitrary", "arbitrary"), vmem_limit_bytes=VMEM_BIG),
        name="ffn_moe",
    )(tile_expert, tile_parts, n_valid, xs, w1, w3, w2)


def _combine_body(start_ref, len_ref, s1_ref, s2_ref, w1_ref, w2_ref, x_ref, gate_ref, os_ref, o_ref,
                  lo_ref, sem):
    i = pl.program_id(0)
    last = pl.num_programs(0) - 1
    slot = i % 2
    other = 1 - slot
    os_at = lambda a, p: os_ref.at[pl.ds(a, p)]

    def gather(tile, buf, start, enable=True):
        plan = [(ln, far, near) for ln, near, far in _tile_plan(tile, len_ref, start_ref)]
        _segment_copies(plan, os_at, lambda a, p: lo_ref.at[buf, pl.ds(a, p)], sem.at[buf], start, enable)

    @pl.when(i == 0)
    def _():
        lo_ref[...] = jnp.zeros_like(lo_ref)
        gather(i, slot, start=True)

    gather(jnp.minimum(i + 1, last), other, start=True, enable=i < last)
    gather(i, slot, start=False)

    lo = lo_ref[slot].astype(BF16)
    lane = lax.broadcasted_iota(I32, (T_ROUTE, S_LOC), 1)
    pick1 = jnp.where(lane == s1_ref[...], 1.0, 0.0).astype(BF16)
    pick2 = jnp.where(lane == s2_ref[...], 1.0, 0.0).astype(BF16)
    y = (w1_ref[...] * jnp.dot(pick1, lo, preferred_element_type=F32)
         + w2_ref[...] * jnp.dot(pick2, lo, preferred_element_type=F32))
    o_ref[...] = x_ref[...] + gate_ref[0] * y


def _combine(x, n, gate, seg_start, seg_len, s1, s2, w1, w2, o_sorted):
    nt = n // T_ROUTE
    col = lambda a: a.reshape(n, 1)
    col_spec = pl.BlockSpec((T_ROUTE, 1), lambda i, *_: (i, 0))
    grid_spec = pltpu.PrefetchScalarGridSpec(
        num_scalar_prefetch=2,
        grid=(nt,),
        in_specs=[col_spec, col_spec, col_spec, col_spec,
                  pl.BlockSpec((T_ROUTE, D), lambda i, *_: (i, 0)),
                  _mod_spec(T_ROUTE),
                  pl.BlockSpec(memory_space=pl.ANY)],
        out_specs=pl.BlockSpec((T_ROUTE, D), lambda i, *_: (i, 0)),
        scratch_shapes=[pltpu.VMEM((2, S_LOC, D), F32), pltpu.SemaphoreType.DMA((2,))],
    )
    return pl.pallas_call(
        _combine_body,
        grid_spec=grid_spec,
        out_shape=jax.ShapeDtypeStruct((n, D), F32),
        compiler_params=pltpu.CompilerParams(
            dimension_semantics=("arbitrary",), vmem_limit_bytes=VMEM_BIG),
        name="moe_combine",
    )(seg_start.reshape(-1), seg_len.reshape(-1), col(s1), col(s2), col(w1), col(w2), x, gate, o_sorted)


def _attn_out_moe(a, w_o, gate1, x, sh, sc, gate, ng, router, w1, w3, w2, layer):
    n = a.shape[0]
    nt = n // T_ROUTE
    x, h, s1, s2, p1, p2, pc = _router(a, w_o, gate1, x, sh, sc, ng, router)
    seg_len = pc[:, :, 0]
    total = jnp.sum(seg_len, axis=0)
    tiles_per = (total + T_MOE - 1) // T_MOE
    tile_ends = jnp.cumsum(tiles_per)
    group_start = (tile_ends - tiles_per) * T_MOE
    seg_start = group_start[None, :] + jnp.cumsum(seg_len, axis=0) - seg_len
    tail_start = group_start + total
    tail_len = tiles_per * T_MOE - total
    n_valid = tile_ends[-1:]
    n_tiles = -(-(2 * n + nt * N_EXPERTS * (SEG_ALIGN - 1)) // T_MOE) + N_EXPERTS
    tile_all = jnp.arange(n_tiles, dtype=I32)
    tile_ids = jnp.minimum(tile_all, n_valid - 1)
    tile_expert = jnp.sum((tile_ids[:, None] >= tile_ends[None, :]).astype(I32), axis=1)
    tile_rows = jnp.clip((group_start + total)[tile_expert] - tile_ids * T_MOE, 0, T_MOE)
    part = T_MOE // MOE_PARTS
    tile_parts = jnp.where(tile_all < n_valid, (tile_rows + part - 1) // part, 0).astype(I32)
    xs = _dispatch(h, s1, s2, seg_start, seg_len, tail_start, tail_len, n_valid, n_tiles)
    o_sorted = _ffn_moe(xs, tile_expert, tile_parts, n_valid, w1, w3, w2, layer)
    return _combine(x, n, gate, seg_start, seg_len, s1, s2, p1, p2, o_sorted)


def _swap_halves(y):
    q = HEAD_DIM // 4
    return jnp.concatenate([y[q:2 * q], y[:q], y[3 * q:], y[2 * q:3 * q]], axis=0)


def _norm_rope_t(zh, c_tab, s_tab):
    ss = jnp.mean(zh * zh, axis=0, keepdims=True)
    yn = zh * lax.rsqrt(ss + EPS)
    return yn * c_tab + _swap_halves(yn) * s_tab


def _qkv_body(x_ref, sh_ref, sc_ref, ng_ref, w_ref, cq_ref, sq_ref, ck_ref, sk_ref, q_ref, k_ref, v_ref):
    h = _modnorm(x_ref[...], ng_ref[...], sc_ref[0], sh_ref[0]).astype(BF16)
    zt = lax.dot_general(w_ref[...], h, (((1,), (1,)), ((), ())), preferred_element_type=F32)
    nq = N_HEADS * HEAD_DIM
    nkv = N_KV * HEAD_DIM
    cq, sq, ck, sk = cq_ref[...], sq_ref[...], ck_ref[...], sk_ref[...]
    for hd in range(N_HEADS):
        lo = hd * HEAD_DIM
        q_ref[lo:lo + HEAD_DIM, :] = _norm_rope_t(zt[lo:lo + HEAD_DIM], cq, sq).astype(BF16)
    kt = jnp.concatenate(
        [_norm_rope_t(zt[nq + kh * HEAD_DIM:nq + (kh + 1) * HEAD_DIM], ck, sk) for kh in range(N_KV)],
        axis=0)
    k_ref[...] = kt.T.astype(BF16)
    v_ref[...] = zt[nq + nkv:].astype(BF16)


def _rope_tables(q_g, k_g):
    half = HEAD_DIM // 2
    quarter = HEAD_DIM // 4
    pos = jnp.arange(SEQ)
    pos_row = (pos // GRID_W).astype(F32)
    pos_col = (pos % GRID_W).astype(F32)
    inv_freq = ROPE_THETA ** (-jnp.arange(0, half, 2, dtype=F32) / half)
    ang_row = inv_freq[:, None] * pos_row[None, :]
    ang_col = inv_freq[:, None] * pos_col[None, :]
    ang = jnp.concatenate([ang_row, ang_row, ang_col, ang_col], axis=0)
    cos = jnp.concatenate([jnp.cos(ang), jnp.ones((HEAD_DIM, T_QKV), F32)], axis=1)
    sin = jnp.concatenate([jnp.sin(ang), jnp.zeros((HEAD_DIM, T_QKV), F32)], axis=1)
    first = ((jnp.arange(HEAD_DIM) % half) < quarter)[:, None]
    sin = jnp.where(first, -sin, sin)

    def tables(g, scale):
        g = g.astype(F32) * scale
        partner = _swap_halves(g[:, None])
        return g[:, None] * cos, partner * sin

    return tables(q_g, HEAD_DIM ** -0.5 * LOG2_E) + tables(k_g, 1.0)


def _qkv(x, sh, sc, ng, w_qkv, q_g, k_g):
    n = x.shape[0]
    nt = n // T_QKV
    nkv = N_KV * HEAD_DIM
    wd = w_qkv.shape[1]
    per_seq = SEQ // T_QKV
    tab_spec = pl.BlockSpec(
        (HEAD_DIM, T_QKV), lambda i: (0, jnp.where(i < N_LAT // T_QKV, i % per_seq, per_seq)))
    mod = _mod_spec(T_QKV)
    return pl.pallas_call(
        _qkv_body,
        grid=(nt,),
        in_specs=[
            pl.BlockSpec((T_QKV, D), lambda i: (i, 0)),
            mod, mod,
            _const_spec((1, D)),
            _const_spec((wd, D)),
            tab_spec, tab_spec, tab_spec, tab_spec,
        ],
        out_specs=[pl.BlockSpec((D, T_QKV), lambda i: (0, i)),
                   pl.BlockSpec((T_QKV, nkv), lambda i: (i, 0)),
                   pl.BlockSpec((nkv, T_QKV), lambda i: (0, i))],
        out_shape=[jax.ShapeDtypeStruct((D, n), BF16),
                   jax.ShapeDtypeStruct((n, nkv), BF16),
                   jax.ShapeDtypeStruct((nkv, n), BF16)],
        compiler_params=pltpu.CompilerParams(
            dimension_semantics=("arbitrary",), vmem_limit_bytes=VMEM_BIG),
        name="attn_qkv",
    )(x, sh, sc, ng.reshape(1, D), w_qkv.T.astype(BF16), *_rope_tables(q_g, k_g))


def _attn_heads(qt_ref, k_all, vt_all, band, sink_ref, o_ref):
    group = N_HEADS // N_KV
    nk = k_all.shape[0]
    ones = jnp.ones((BF16_ROWS, nk), BF16)
    zeros = jnp.zeros((HEAD_DIM, group * T_Q), BF16)
    outs = []
    scores = []
    probs = []
    for kh in range(N_KV):
        q4 = jnp.concatenate(
            [qt_ref[(kh * group + g) * HEAD_DIM:(kh * group + g + 1) * HEAD_DIM, :] for g in range(group)],
            axis=1)
        qpad = jnp.concatenate([q4, zeros] if kh % 2 == 0 else [zeros, q4], axis=0)
        k2 = k_all[:, (kh // 2) * LANES:(kh // 2 + 1) * LANES]
        scores.append(jnp.dot(k2, qpad, preferred_element_type=F32))
    for kh in range(N_KV):
        st = scores[kh]
        blocks = [st[c * T_Q:(c + 1) * T_Q] for c in range(nk // T_Q)]
        if band is not None:
            blocks[0] = jnp.where(band[0], blocks[0], NEG_INF)
            blocks[2] = jnp.where(band[1], blocks[2], NEG_INF)
        best = blocks[0]
        for blk in blocks[1:]:
            best = jnp.maximum(best, blk)
        sink = jnp.concatenate(
            [jnp.full((1, T_Q), sink_ref[kh * group + g] * LOG2_E, F32) for g in range(group)], axis=1)
        m = jnp.maximum(jnp.max(best, axis=0, keepdims=True), sink)
        pt = jnp.concatenate([jnp.exp2(blk - m).astype(BF16) for blk in blocks], axis=0)
        probs.append((pt, jnp.exp2(sink - m)))
    for kh in range(N_KV):
        pt, sink_term = probs[kh]
        vt_aug = jnp.concatenate([vt_all[kh * HEAD_DIM:(kh + 1) * HEAD_DIM, :], ones], axis=0)
        ot = jnp.dot(vt_aug, pt, preferred_element_type=F32)
        den = ot[HEAD_DIM:HEAD_DIM + 1] + sink_term
        o = ot[:HEAD_DIM] / den
        outs.extend(o[:, g * T_Q:(g + 1) * T_Q] for g in range(group))
    o_ref[...] = jnp.concatenate(outs, axis=0).T.astype(BF16)


def _attn_body(sink_ref, q_ref, kp_ref, kc_ref, kn_ref, kx_ref, vp_ref, vc_ref, vn_ref, vx_ref, o_ref,
               *, n_q_blocks):
    iq = pl.program_id(1)
    per_seq = SEQ // T_Q
    group = N_HEADS // N_KV

    @pl.when(iq < per_seq)
    def _():
        k_all = jnp.concatenate([kp_ref[...], kc_ref[...], kn_ref[...], kx_ref[...]], axis=0)
        vt_all = jnp.concatenate([vp_ref[...], vc_ref[...], vn_ref[...], vx_ref[...]], axis=1)
        c = lax.broadcasted_iota(I32, (T_Q, group * T_Q), 0)
        r = lax.broadcasted_iota(I32, (T_Q, group * T_Q), 1) % T_Q
        far = 2 * T_Q
        mask_prev = c >= r + jnp.where(iq > 0, 0, far)
        mask_next = c <= r - jnp.where(iq < per_seq - 1, 0, far)
        _attn_heads(q_ref, k_all, vt_all, (mask_prev, mask_next), sink_ref, o_ref)

    if n_q_blocks > per_seq:
        @pl.when(iq >= per_seq)
        def _():
            _attn_heads(q_ref, kx_ref[...], vx_ref[...], None, sink_ref, o_ref)


def _attention(qt, k, vt, sink, need_ctx):
    assert T_Q == WINDOW
    per_seq = SEQ // T_Q
    ctx_blocks = CTX_LEN // T_Q
    n_q_blocks = per_seq + (ctx_blocks if need_ctx else 0)
    n_out = N_ALL if need_ctx else N_LAT
    lat_blocks = N_LAT // T_Q
    kw = k.shape[1]

    def q_blk(b, iq):
        return jnp.where(iq < per_seq, b * per_seq + iq, lat_blocks + b * ctx_blocks + (iq - per_seq))

    def win_blk(b, iq, off):
        return b * per_seq + jnp.clip(iq + off, 0, per_seq - 1)

    ctx_blk = lambda b: N_LAT // CTX_LEN + b
    k_win = lambda off: pl.BlockSpec((T_Q, kw), lambda b, iq, *_: (win_blk(b, iq, off), 0))
    v_win = lambda off: pl.BlockSpec((kw, T_Q), lambda b, iq, *_: (0, win_blk(b, iq, off)))
    k_ctx = pl.BlockSpec((CTX_LEN, kw), lambda b, iq, *_: (ctx_blk(b), 0))
    v_ctx = pl.BlockSpec((kw, CTX_LEN), lambda b, iq, *_: (0, ctx_blk(b)))
    grid_spec = pltpu.PrefetchScalarGridSpec(
        num_scalar_prefetch=1,
        grid=(NB, n_q_blocks),
        in_specs=[pl.BlockSpec((D, T_Q), lambda b, iq, *_: (0, q_blk(b, iq))),
                  k_win(-1), k_win(0), k_win(1), k_ctx, v_win(-1), v_win(0), v_win(1), v_ctx],
        out_specs=pl.BlockSpec((T_Q, D), lambda b, iq, *_: (q_blk(b, iq), 0)),
    )
    return pl.pallas_call(
        functools.partial(_attn_body, n_q_blocks=n_q_blocks),
        grid_spec=grid_spec,
        out_shape=jax.ShapeDtypeStruct((n_out, D), BF16),
        compiler_params=pltpu.CompilerParams(
            dimension_semantics=("arbitrary", "arbitrary"), vmem_limit_bytes=VMEM_BIG),
        name="attn_core",
    )(sink.astype(F32), qt, k, k, k, k, vt, vt, vt, vt)


def kernel(x, c, ctx, c_ctx, ada_w, ada_b, norm_mix_g, norm_ffn_g, ev_w_in, ev_ln_g, ev_ln_b, ev_ws,
           ev_bs, ev_conv_w, ev_conv_b, ev_cnorm_g, ev_w_out, od_w_qkv, od_q_g, od_k_g, od_sink, od_w_o,
           ff_w1, ff_w3, ff_w2, moe_router, moe_w1, moe_w3, moe_w2):
    assert x.shape == (NB, SEQ, D) and ctx.shape == (NB, CTX_LEN, D)
    mods = _ada_mods(c, c_ctx, ada_w, ada_b)
    xa = None
    ev_w_in_b, ev_w_out_b = ev_w_in.astype(BF16), ev_w_out.astype(BF16)
    for li in range(DEPTH):
        need_ctx = li < DEPTH - 1
        j = li // 2
        sh1, sc1, g1, sh2, sc2, g2 = mods[li]
        if li % 2 == 0:
            xin = (x.reshape(N_LAT, D), ctx.reshape(N_CTX, D)) if li == 0 else (xa, None)
            xa = _even_mixer(*xin, sh1, sc1, g1, norm_mix_g[li], ev_w_in_b, ev_ln_g[j], ev_ln_b[j],
                             ev_ws[j], ev_bs[j], ev_conv_w[j], ev_conv_b[j], ev_cnorm_g[j], ev_w_out_b, j)
            xa = _ffn_dense(xa, sh2, sc2, g2, norm_ffn_g[li], ff_w1, ff_w3, ff_w2, j)
        else:
            qt, k, vt = _qkv(xa, sh1, sc1, norm_mix_g[li], od_w_qkv[j], od_q_g[j], od_k_g[j])
            o = _attention(qt, k, vt, od_sink[j], need_ctx)
            xa = _attn_out_moe(o, od_w_o[j], g1, xa, sh2, sc2, g2, norm_ffn_g[li], moe_router[j],
                               moe_w1, moe_w3, moe_w2, j)
    return xa[:N_LAT].reshape(NB, SEQ, D)
```

```python
import functools

import jax
import jax.numpy as jnp
from jax import lax
from jax.experimental import pallas as pl
from jax.experimental.pallas import tpu as pltpu

F32 = jnp.float32
BF16 = jnp.bfloat16
I32 = jnp.int32

D = 1024
NB = 8
SEQ = 2048
CTX_LEN = 256
DEPTH = 4
GRID_W = 64
CHUNK = 128
A_GROUPS = 8
CONV_W = 31
N_HEADS = 16
N_KV = 4
HEAD_DIM = 64
WINDOW = 128
ROPE_THETA = 10000.0
D_FF = 3584
N_EXPERTS = 8
EPS = 1e-6
NEG_INF = -1e30
LOG2_E = 1.4426950408889634

N_LAT = NB * SEQ
N_CTX = NB * CTX_LEN
N_ALL = N_LAT + N_CTX
CTX_MOD_ROW = NB

LANES = 128
SUBLANES = 8
BF16_ROWS = 16

T_EVEN = 256
EVEN_PAIR = 2
HALO = BF16_ROWS
T_FFN = 1024
T_FF = 512
T_ROUTE = 512
T_MOE = 1024
MOE_PARTS = 8
SEG_ALIGN = SUBLANES
S_LOC = 2 * T_ROUTE + N_EXPERTS * SEG_ALIGN
SEG_SIZES = tuple(T_ROUTE >> s for s in range(T_ROUTE.bit_length()) if T_ROUTE >> s >= SEG_ALIGN)
assert SEG_SIZES[-1] == SEG_ALIGN
T_QKV = 512
T_Q = 128

VMEM_BIG = 52 * 1024 * 1024


def _mod_spec(tile):
    per = SEQ // tile
    return pl.BlockSpec((1, 1, D), lambda i, *_: (jnp.minimum(i // per, CTX_MOD_ROW), 0, 0))


def _const_spec(shape):
    nd = len(shape)
    return pl.BlockSpec(shape, lambda *_: (0,) * nd)


def _modnorm(x, g, sc, sh):
    ms = jnp.mean(x * x, axis=-1, keepdims=True)
    return (x * lax.rsqrt(ms + EPS) * g) * (1.0 + sc) + sh


def _silu(x):
    return x * jax.nn.sigmoid(x)


def _ada_body(c_ref, w_ref, b_ref, o_ref):
    a = _silu(c_ref[...])
    o_ref[0] = jnp.dot(a.astype(BF16), w_ref[0].astype(BF16), preferred_element_type=F32) + b_ref[0]


def _ada_mods(c, c_ctx, ada_w, ada_b):
    rows = BF16_ROWS
    cc = jnp.concatenate([c, c_ctx[None, :], jnp.zeros((rows - NB - 1, D), F32)], axis=0)
    out = pl.pallas_call(
        _ada_body,
        grid=(DEPTH, 6),
        in_specs=[
            pl.BlockSpec((rows, D), lambda l, n: (0, 0)),
            pl.BlockSpec((1, D, D), lambda l, n: (l, 0, n)),
            pl.BlockSpec((1, 1, D), lambda l, n: (l, 0, n)),
        ],
        out_specs=pl.BlockSpec((1, rows, D), lambda l, n: (l, 0, n)),
        out_shape=jax.ShapeDtypeStruct((DEPTH, rows, 6 * D), F32),
        name="ada_mod",
    )(cc, ada_w, ada_b.reshape(DEPTH, 1, 6 * D))
    m = out[:, :NB + 1].reshape(DEPTH, NB + 1, 6, 1, D)
    return [[m[l, :, k] for k in range(6)] for l in range(DEPTH)]


def _even_body(x_ref, xc_ref, xp_ref, xn_ref, sh_ref, sc_ref, gate_ref, ng_ref, win_ref, lng_ref, lnb_ref,
               ws_ref, bs_ref, cw_ref, cb_ref, cng_ref, wout_ref, o_ref, gext_ref, *, split):
    step = pl.program_id(0)
    per_seq = SEQ // T_EVEN
    ext = T_EVEN + 2 * HALO
    halves = range(EVEN_PAIR)
    x_all = x_ref[...]
    if split:
        x_all = jnp.where(step < N_LAT // (EVEN_PAIR * T_EVEN), x_all, xc_ref[...])
    tiles = [x_all[hf * T_EVEN:(hf + 1) * T_EVEN] for hf in halves]
    befores = [xp_ref[...]] + [tiles[hf - 1][T_EVEN - HALO:] for hf in halves[1:]]
    afters = [tiles[hf + 1][:HALO] for hf in halves[:-1]] + [xn_ref[...]]

    hs = [_modnorm(jnp.concatenate([befores[hf], tiles[hf], afters[hf]], axis=0),
                   ng_ref[...], sc_ref[0], sh_ref[0]).astype(BF16) for hf in halves]
    zbs = [jnp.dot(hs[hf], win_ref[0, :, 2 * D:], preferred_element_type=F32) for hf in halves]
    zas = [jnp.dot(hs[hf][HALO:HALO + T_EVEN], win_ref[0, :, :2 * D], preferred_element_type=F32)
           for hf in halves]

    for hf in halves:
        i = step * EVEN_PAIR + hf
        is_lat = i < N_LAT // T_EVEN
        pos = i % per_seq
        is_start = jnp.logical_or(jnp.logical_not(is_lat), pos == 0)
        is_end = jnp.logical_or(jnp.logical_not(is_lat), pos == per_seq - 1)
        gg = zbs[hf][:, :D] * jax.nn.sigmoid(zbs[hf][:, D:])
        row = lax.broadcasted_iota(I32, (ext, 1), 0)
        lo = jnp.where(is_start, HALO, 0)
        hi = jnp.where(is_end, T_EVEN + HALO, ext)
        gg = jnp.where(jnp.logical_and(row >= lo, row < hi), gg, 0.0)
        for cbk in range(D // LANES):
            gext_ref[hf, cbk] = gg[:, cbk * LANES:(cbk + 1) * LANES]

    rows_per = 64
    gdim = D // A_GROUPS
    for hf in halves:
        cols = []
        for cbk in range(D // LANES):
            blocks = []
            for rb in range(T_EVEN // rows_per):
                acc = jnp.zeros((rows_per, LANES), F32)
                for k in range(CONV_W):
                    off = rb * rows_per + k + HALO - CONV_W // 2
                    acc = acc + cw_ref[cbk, pl.ds(k, 1), :] * gext_ref[hf, cbk, pl.ds(off, rows_per), :]
                blocks.append(acc)
            cols.append(jnp.concatenate(blocks, axis=0))

        cv = jnp.concatenate(cols, axis=1) + cb_ref[...]
        ms = jnp.mean(cv * cv, axis=-1, keepdims=True)
        yb = _silu(cv * lax.rsqrt(ms + EPS) * cng_ref[...])

        u = jax.nn.gelu(zas[hf][:, :D])
        v = jax.nn.gelu(zas[hf][:, D:])
        mu = jnp.mean(v, axis=-1, keepdims=True)
        vc = v - mu
        var = jnp.mean(vc * vc, axis=-1, keepdims=True)
        vn = (vc * lax.rsqrt(var + EPS) * lng_ref[...] + lnb_ref[...]).astype(BF16)
        chunks = []
        for ck in range(T_EVEN // CHUNK):
            blocks = [
                jnp.dot(ws_ref[g], vn[ck * CHUNK:(ck + 1) * CHUNK, g * gdim:(g + 1) * gdim],
                        preferred_element_type=F32)
                for g in range(A_GROUPS)
            ]
            chunks.append(jnp.concatenate(blocks, axis=1) + bs_ref[...])
        ya = u * jnp.concatenate(chunks, axis=0)

        y = (jnp.dot(ya.astype(BF16), wout_ref[0, :D], preferred_element_type=F32)
             + jnp.dot(yb.astype(BF16), wout_ref[0, D:], preferred_element_type=F32))
        o_ref[hf * T_EVEN:(hf + 1) * T_EVEN, :] = tiles[hf] + gate_ref[0] * y


def _even_mixer(x, x_ctx, sh, sc, gate, ng, w_in, ln_g, ln_b, ws, bs, conv_w, conv_b, cn_g, w_out, j):
    split = x_ctx is not None
    n = x.shape[0] + (x_ctx.shape[0] if split else 0)
    rows = EVEN_PAIR * T_EVEN
    nt = n // rows
    hb = rows // HALO
    last = x.shape[0] // HALO - 1
    x_steps = x.shape[0] // rows
    if split:
        main_specs = [pl.BlockSpec((rows, D), lambda i: (jnp.minimum(i, x_steps - 1), 0)),
                      pl.BlockSpec((rows, D), lambda i: (jnp.maximum(i - x_steps, 0), 0))]
        x_second = x_ctx
    else:
        main_specs = [pl.BlockSpec((rows, D), lambda i: (i, 0)), _const_spec((HALO, D))]
        x_second = x[:HALO]
    ncb = D // LANES
    bs_full = jnp.repeat(bs.T, D // A_GROUPS, axis=1)
    taps = -(-CONV_W // SUBLANES) * SUBLANES
    cw = jnp.pad(conv_w, ((0, taps - CONV_W), (0, 0))).reshape(taps, ncb, LANES).transpose(1, 0, 2)
    mod = _mod_spec(rows)
    row1 = lambda a: a.reshape(1, D)
    return pl.pallas_call(
        functools.partial(_even_body, split=split),
        grid=(nt,),
        in_specs=main_specs + [
            pl.BlockSpec((HALO, D), lambda i: (jnp.clip(i * hb - 1, 0, last), 0)),
            pl.BlockSpec((HALO, D), lambda i: (jnp.minimum((i + 1) * hb, last), 0)),
            mod, mod, mod,
            _const_spec((1, D)),
            pl.BlockSpec((1, D, 4 * D), lambda i: (j, 0, 0)),
            _const_spec((1, D)), _const_spec((1, D)),
            _const_spec((A_GROUPS, CHUNK, CHUNK)),
            _const_spec((CHUNK, D)),
            _const_spec((ncb, taps, LANES)),
            _const_spec((1, D)), _const_spec((1, D)),
            pl.BlockSpec((1, 2 * D, D), lambda i: (j, 0, 0)),
        ],
        out_specs=pl.BlockSpec((rows, D), lambda i: (i, 0)),
        out_shape=jax.ShapeDtypeStruct((n, D), F32),
        scratch_shapes=[
            pltpu.VMEM((EVEN_PAIR, ncb, T_EVEN + 2 * HALO, LANES), F32),
        ],
        compiler_params=pltpu.CompilerParams(
            dimension_semantics=("arbitrary",), vmem_limit_bytes=VMEM_BIG),
        name="even_mixer",
    )(x, x_second, x, x, sh, sc, gate, row1(ng), w_in, row1(ln_g), row1(ln_b), ws.astype(BF16),
      bs_full, cw, row1(conv_b), row1(cn_g), w_out)


def _ffn_dense_body(x_ref, sh_ref, sc_ref, gate_ref, ng_ref, w1_ref, w3_ref, w2_ref, o_ref,
                    h_ref, acc_ref):
    j = pl.program_id(1)

    @pl.when(jnp.logical_and(pl.program_id(0) == 0, j == 0))
    def _():
        acc_ref[...] = jnp.zeros_like(acc_ref)

    @pl.when(j == 0)
    def _():
        h_ref[...] = _modnorm(x_ref[...], ng_ref[...], sc_ref[0], sh_ref[0]).astype(BF16)

    h = h_ref[...]
    a = jnp.dot(h, w1_ref[0].astype(BF16), preferred_element_type=F32)
    b = jnp.dot(h, w3_ref[0].astype(BF16), preferred_element_type=F32)
    t = (_silu(a) * b).astype(BF16)
    prev = jnp.where(j == 0, 0.0, acc_ref[...])
    acc_ref[...] = prev + jnp.dot(t, w2_ref[0].astype(BF16), preferred_element_type=F32)

    @pl.when(j == pl.num_programs(1) - 1)
    def _():
        o_ref[...] = x_ref[...] + gate_ref[0] * acc_ref[...]


def _ffn_dense(x, sh, sc, gate, ng, w1, w3, w2, layer):
    n = x.shape[0]
    mod = _mod_spec(T_FFN)
    return pl.pallas_call(
        _ffn_dense_body,
        grid=(n // T_FFN, D_FF // T_FF),
        in_specs=[
            pl.BlockSpec((T_FFN, D), lambda i, j: (i, 0)),
            mod, mod, mod,
            _const_spec((1, D)),
            pl.BlockSpec((1, D, T_FF), lambda i, j: (layer, 0, j)),
            pl.BlockSpec((1, D, T_FF), lambda i, j: (layer, 0, j)),
            pl.BlockSpec((1, T_FF, D), lambda i, j: (layer, j, 0)),
        ],
        out_specs=pl.BlockSpec((T_FFN, D), lambda i, j: (i, 0)),
        out_shape=jax.ShapeDtypeStruct((n, D), F32),
        scratch_shapes=[pltpu.VMEM((T_FFN, D), BF16), pltpu.VMEM((T_FFN, D), F32)],
        compiler_params=pltpu.CompilerParams(
            dimension_semantics=("arbitrary", "arbitrary"), vmem_limit_bytes=VMEM_BIG),
        name="ffn_dense",
    )(x, sh, sc, gate, ng.reshape(1, D), w1, w3, w2)


def _router_body(a_ref, wo_ref, g1_ref, x_ref, sh_ref, sc_ref, ng_ref, rt_ref,
                 xo_ref, hb_ref, s1_ref, s2_ref, w1_ref, w2_ref, pc_ref):
    xn = x_ref[...] + g1_ref[0] * jnp.dot(a_ref[...], wo_ref[...], preferred_element_type=F32)
    xo_ref[...] = xn
    h = _modnorm(xn, ng_ref[...], sc_ref[0], sh_ref[0])
    hb_ref[...] = h.astype(BF16)
    lg = lax.dot_general(rt_ref[...], h, (((1,), (1,)), ((), ())),
                         precision=lax.Precision.HIGHEST, preferred_element_type=F32)
    e = lax.broadcasted_iota(I32, lg.shape, 0).astype(F32)
    big = float(N_EXPERTS)
    m1 = jnp.max(lg, axis=0, keepdims=True)
    i1 = jnp.min(jnp.where(lg == m1, e, big), axis=0, keepdims=True)
    lg2 = jnp.where(e == i1, -jnp.inf, lg)
    m2 = jnp.max(lg2, axis=0, keepdims=True)
    i2 = jnp.min(jnp.where(lg2 == m2, e, big), axis=0, keepdims=True)
    e2 = jnp.exp(m2 - m1)
    den = 1.0 + e2
    w1_ref[0] = 1.0 / den
    w2_ref[0] = e2 / den

    sel1 = e == i1
    sel2 = e == i2
    member = jnp.where(jnp.logical_or(sel1, sel2), 1.0, 0.0)
    t = lg.shape[1]
    before = (lax.broadcasted_iota(I32, (t, t), 0) < lax.broadcasted_iota(I32, (t, t), 1))
    tri = jnp.where(before, 1.0, 0.0).astype(BF16)
    rank = jnp.dot(member.astype(BF16), tri, preferred_element_type=F32)
    count = jnp.sum(member, axis=1, keepdims=True)
    padded = jnp.ceil(count * (1.0 / SEG_ALIGN)) * SEG_ALIGN
    e_col = lax.broadcasted_iota(I32, (N_EXPERTS, 1), 0)
    seg_start = jnp.zeros((N_EXPERTS, 1), F32)
    for ex in range(1, N_EXPERTS):
        below = jnp.sum(jnp.where(e_col < ex, padded, 0.0), axis=0, keepdims=True)
        seg_start = jnp.where(e_col == ex, below, seg_start)
    slot = rank + seg_start
    s1 = jnp.sum(jnp.where(sel1, slot, 0.0), axis=0, keepdims=True).astype(I32)
    s2 = jnp.sum(jnp.where(sel2, slot, 0.0), axis=0, keepdims=True).astype(I32)
    s1_ref[0] = s1
    s2_ref[0] = s2
    pc_ref[0] = jnp.broadcast_to(padded.astype(I32), (N_EXPERTS, LANES))


def _router(a, w_o, gate1, x, sh, sc, ng, router):
    n = a.shape[0]
    nt = n // T_ROUTE
    mod = _mod_spec(T_ROUTE)
    row_spec = pl.BlockSpec((T_ROUTE, D), lambda i: (i, 0))
    vec_spec = pl.BlockSpec((1, 1, T_ROUTE), lambda i: (i, 0, 0))
    vec_i = jax.ShapeDtypeStruct((nt, 1, T_ROUTE), I32)
    vec_f = jax.ShapeDtypeStruct((nt, 1, T_ROUTE), F32)
    return pl.pallas_call(
        _router_body,
        grid=(nt,),
        in_specs=[
            row_spec,
            _const_spec((D, D)),
            mod,
            row_spec,
            mod, mod,
            _const_spec((1, D)),
            _const_spec((N_EXPERTS, D)),
        ],
        out_specs=[row_spec, row_spec, vec_spec, vec_spec, vec_spec, vec_spec,
                   pl.BlockSpec((1, N_EXPERTS, LANES), lambda i: (i, 0, 0))],
        out_shape=[jax.ShapeDtypeStruct((n, D), F32), jax.ShapeDtypeStruct((n, D), BF16),
                   vec_i, vec_i, vec_f, vec_f, jax.ShapeDtypeStruct((nt, N_EXPERTS, LANES), I32)],
        compiler_params=pltpu.CompilerParams(
            dimension_semantics=("arbitrary",), vmem_limit_bytes=VMEM_BIG),
        name="moe_router",
    )(a, w_o.astype(BF16), gate1, x, sh, sc, ng.reshape(1, D), router.T)


def _dispatch_body(start_ref, len_ref, tail_start_ref, tail_len_ref, nv_ref,
                   h_ref, s1_ref, s2_ref, xs_ref, lbuf_ref, zbuf_ref, sem,
                   *, n_tiles, min_tiles):
    i = pl.program_id(0)
    slot = i % 2
    other = 1 - slot
    h = h_ref[...]
    rows = lax.broadcasted_iota(I32, (S_LOC, T_ROUTE), 0)
    perm = jnp.where(jnp.logical_or(rows == s1_ref[0], rows == s2_ref[0]), 1.0, 0.0).astype(BF16)
    lbuf_ref[slot] = jnp.dot(perm, h, preferred_element_type=F32)

    xs_at = lambda a, p: xs_ref.at[pl.ds(a, p)]
    _segment_copies(_tile_plan(i, len_ref, start_ref), lambda a, p: lbuf_ref.at[slot, pl.ds(a, p)], xs_at,
                    sem.at[slot], start=True)
    _segment_copies(_tile_plan(jnp.maximum(i - 1, 0), len_ref, start_ref),
                    lambda a, p: lbuf_ref.at[other, pl.ds(a, p)], xs_at, sem.at[other], start=False,
                    enable=i > 0)

    @pl.when(i == pl.num_programs(0) - 1)
    def _():
        _segment_copies(_tile_plan(i, len_ref, start_ref), lambda a, p: lbuf_ref.at[slot, pl.ds(a, p)],
                        xs_at, sem.at[slot], start=False)
        zbuf_ref[...] = jnp.zeros_like(zbuf_ref)
        tails = [(tail_len_ref[ex], 0, tail_start_ref[ex]) for ex in range(N_EXPERTS)]
        zero_at = lambda a, p: zbuf_ref.at[pl.ds(a, p)]
        _segment_copies(tails, zero_at, xs_at, sem.at[slot], start=True)
        _segment_copies(tails, zero_at, xs_at, sem.at[slot], start=False)
        for tile in range(min_tiles, n_tiles):
            @pl.when(tile >= nv_ref[0])
            def _(tile=tile):
                cp = pltpu.make_async_copy(zbuf_ref, xs_ref.at[pl.ds(tile * T_MOE, T_MOE)], sem.at[slot])
                cp.start()
                cp.wait()


def _dispatch(h, s1, s2, seg_start, seg_len, tail_start, tail_len, n_valid, n_tiles):
    n = h.shape[0]
    nt = n // T_ROUTE
    vec_spec = pl.BlockSpec((1, 1, T_ROUTE), lambda i, *_: (i, 0, 0))
    grid_spec = pltpu.PrefetchScalarGridSpec(
        num_scalar_prefetch=5,
        grid=(nt,),
        in_specs=[pl.BlockSpec((T_ROUTE, D), lambda i, *_: (i, 0)),
                  vec_spec, vec_spec],
        out_specs=pl.BlockSpec(memory_space=pl.ANY),
        scratch_shapes=[pltpu.VMEM((2, S_LOC, D), F32), pltpu.VMEM((T_MOE, D), F32),
                        pltpu.SemaphoreType.DMA((2,))],
    )
    return pl.pallas_call(
        functools.partial(_dispatch_body, n_tiles=n_tiles, min_tiles=(2 * n) // T_MOE),
        grid_spec=grid_spec,
        out_shape=jax.ShapeDtypeStruct((n_tiles * T_MOE, D), F32),
        compiler_params=pltpu.CompilerParams(
            dimension_semantics=("arbitrary",), vmem_limit_bytes=VMEM_BIG),
        name="moe_dispatch",
    )(seg_start.reshape(-1), seg_len.reshape(-1), tail_start, tail_len, n_valid, h, s1, s2)


def _tile_plan(tile, len_ref, far_ref):
    near = 0
    plan = []
    for ex in range(N_EXPERTS):
        ln = len_ref[tile * N_EXPERTS + ex]
        plan.append((ln, near, far_ref[tile * N_EXPERTS + ex]))
        near = near + ln
    return plan


def _segment_copies(plan, src_at, dst_at, sem, start, enable=True):
    for ln, src, dst in plan:
        for p in SEG_SIZES:
            off = jnp.bitwise_and(ln, -(2 * p))

            @pl.when(jnp.logical_and(jnp.bitwise_and(ln, p) != 0, enable))
            def _(off=off, src=src, dst=dst, p=p):
                cp = pltpu.make_async_copy(src_at(pl.multiple_of(src + off, SEG_ALIGN), p),
                                           dst_at(pl.multiple_of(dst + off, SEG_ALIGN), p), sem)
                if start:
                    cp.start()
                else:
                    cp.wait()


def _ffn_moe_body(te_ref, parts_ref, nv_ref, xs_ref, w1_ref, w3_ref, w2_ref, o_ref, acc_ref):
    del te_ref
    i = pl.program_id(0)
    j = pl.program_id(1)
    valid = i < nv_ref[0]

    @pl.when(jnp.logical_and(i == 0, j == 0))
    def _():
        acc_ref[...] = jnp.zeros_like(acc_ref)

    for parts in range(1, MOE_PARTS + 1):
        @pl.when(parts_ref[i] == parts)
        def _(rows=parts * (T_MOE // MOE_PARTS)):
            h = xs_ref[:rows].astype(BF16)
            a = jnp.dot(h, w1_ref[0, 0].astype(BF16), preferred_element_type=F32)
            b = jnp.dot(h, w3_ref[0, 0].astype(BF16), preferred_element_type=F32)
            t = (_silu(a) * b).astype(BF16)
            prev = jnp.where(j == 0, 0.0, acc_ref[:rows])
            acc_ref[:rows] = prev + jnp.dot(t, w2_ref[0, 0].astype(BF16), preferred_element_type=F32)
            if rows < T_MOE:
                @pl.when(j == 0)
                def _():
                    acc_ref[rows:] = jnp.zeros((T_MOE - rows, D), F32)

    last = j == pl.num_programs(1) - 1

    @pl.when(jnp.logical_and(valid, last))
    def _():
        o_ref[...] = acc_ref[...]

    @pl.when(jnp.logical_and(jnp.logical_not(valid), last))
    def _():
        o_ref[...] = jnp.zeros_like(o_ref)


def _ffn_moe(xs, tile_expert, tile_parts, n_valid, w1, w3, w2, layer):
    nj = D_FF // T_FF

    def jj(i, j, nv):
        return jnp.where(i < nv[0], j, nj - 1)

    grid_spec = pltpu.PrefetchScalarGridSpec(
        num_scalar_prefetch=3,
        grid=(xs.shape[0] // T_MOE, nj),
        in_specs=[
            pl.BlockSpec((T_MOE, D), lambda i, j, te, tp, nv: (i, 0)),
            pl.BlockSpec((1, 1, D, T_FF), lambda i, j, te, tp, nv: (layer, te[i], 0, jj(i, j, nv))),
            pl.BlockSpec((1, 1, D, T_FF), lambda i, j, te, tp, nv: (layer, te[i], 0, jj(i, j, nv))),
            pl.BlockSpec((1, 1, T_FF, D), lambda i, j, te, tp, nv: (layer, te[i], jj(i, j, nv), 0)),
        ],
        out_specs=pl.BlockSpec((T_MOE, D), lambda i, j, te, tp, nv: (i, 0)),
        scratch_shapes=[pltpu.VMEM((T_MOE, D), F32)],
    )
    return pl.pallas_call(
        _ffn_moe_body,
        grid_spec=grid_spec,
        out_shape=jax.ShapeDtypeStruct(xs.shape, F32),
        compiler_params=pltpu.CompilerParams(
            dimension_semantics=("arbitrary", "arbitrary"), vmem_limit_bytes=VMEM_BIG),
        name="ffn_moe",
    )(tile_expert, tile_parts, n_valid, xs, w1, w3, w2)


def _combine_body(start_ref, len_ref, s1_ref, s2_ref, w1_ref, w2_ref, x_ref, gate_ref, os_ref, o_ref,
                  lo_ref, sem):
    i = pl.program_id(0)
    last = pl.num_programs(0) - 1
    slot = i % 2
    other = 1 - slot
    os_at = lambda a, p: os_ref.at[pl.ds(a, p)]

    def gather(tile, buf, start, enable=True):
        plan = [(ln, far, near) for ln, near, far in _tile_plan(tile, len_ref, start_ref)]
        _segment_copies(plan, os_at, lambda a, p: lo_ref.at[buf, pl.ds(a, p)], sem.at[buf], start, enable)

    @pl.when(i == 0)
    def _():
        lo_ref[...] = jnp.zeros_like(lo_ref)
        gather(i, slot, start=True)

    gather(jnp.minimum(i + 1, last), other, start=True, enable=i < last)
    gather(i, slot, start=False)

    lo = lo_ref[slot].astype(BF16)
    lane = lax.broadcasted_iota(I32, (T_ROUTE, S_LOC), 1)
    pick1 = jnp.where(lane == s1_ref[...], 1.0, 0.0).astype(BF16)
    pick2 = jnp.where(lane == s2_ref[...], 1.0, 0.0).astype(BF16)
    y = (w1_ref[...] * jnp.dot(pick1, lo, preferred_element_type=F32)
         + w2_ref[...] * jnp.dot(pick2, lo, preferred_element_type=F32))
    o_ref[...] = x_ref[...] + gate_ref[0] * y


def _combine(x, n, gate, seg_start, seg_len, s1, s2, w1, w2, o_sorted):
    nt = n // T_ROUTE
    col = lambda a: a.reshape(n, 1)
    col_spec = pl.BlockSpec((T_ROUTE, 1), lambda i, *_: (i, 0))
    grid_spec = pltpu.PrefetchScalarGridSpec(
        num_scalar_prefetch=2,
        grid=(nt,),
        in_specs=[col_spec, col_spec, col_spec, col_spec,
                  pl.BlockSpec((T_ROUTE, D), lambda i, *_: (i, 0)),
                  _mod_spec(T_ROUTE),
                  pl.BlockSpec(memory_space=pl.ANY)],
        out_specs=pl.BlockSpec((T_ROUTE, D), lambda i, *_: (i, 0)),
        scratch_shapes=[pltpu.VMEM((2, S_LOC, D), F32), pltpu.SemaphoreType.DMA((2,))],
    )
    return pl.pallas_call(
        _combine_body,
        grid_spec=grid_spec,
        out_shape=jax.ShapeDtypeStruct((n, D), F32),
        compiler_params=pltpu.CompilerParams(
            dimension_semantics=("arbitrary",), vmem_limit_bytes=VMEM_BIG),
        name="moe_combine",
    )(seg_start.reshape(-1), seg_len.reshape(-1), col(s1), col(s2), col(w1), col(w2), x, gate, o_sorted)


def _attn_out_moe(a, w_o, gate1, x, sh, sc, gate, ng, router, w1, w3, w2, layer):
    n = a.shape[0]
    nt = n // T_ROUTE
    x, h, s1, s2, p1, p2, pc = _router(a, w_o, gate1, x, sh, sc, ng, router)
    seg_len = pc[:, :, 0]
    total = jnp.sum(seg_len, axis=0)
    tiles_per = (total + T_MOE - 1) // T_MOE
    tile_ends = jnp.cumsum(tiles_per)
    group_start = (tile_ends - tiles_per) * T_MOE
    seg_start = group_start[None, :] + jnp.cumsum(seg_len, axis=0) - seg_len
    tail_start = group_start + total
    tail_len = tiles_per * T_MOE - total
    n_valid = tile_ends[-1:]
    n_tiles = -(-(2 * n + nt * N_EXPERTS * (SEG_ALIGN - 1)) // T_MOE) + N_EXPERTS
    tile_all = jnp.arange(n_tiles, dtype=I32)
    tile_ids = jnp.minimum(tile_all, n_valid - 1)
    tile_expert = jnp.sum((tile_ids[:, None] >= tile_ends[None, :]).astype(I32), axis=1)
    tile_rows = jnp.clip((group_start + total)[tile_expert] - tile_ids * T_MOE, 0, T_MOE)
    part = T_MOE // MOE_PARTS
    tile_parts = jnp.where(tile_all < n_valid, (tile_rows + part - 1) // part, 0).astype(I32)
    xs = _dispatch(h, s1, s2, seg_start, seg_len, tail_start, tail_len, n_valid, n_tiles)
    o_sorted = _ffn_moe(xs, tile_expert, tile_parts, n_valid, w1, w3, w2, layer)
    return _combine(x, n, gate, seg_start, seg_len, s1, s2, p1, p2, o_sorted)


def _swap_halves(y):
    q = HEAD_DIM // 4
    return jnp.concatenate([y[q:2 * q], y[:q], y[3 * q:], y[2 * q:3 * q]], axis=0)


def _norm_rope_t(zh, c_tab, s_tab):
    ss = jnp.mean(zh * zh, axis=0, keepdims=True)
    yn = zh * lax.rsqrt(ss + EPS)
    return yn * c_tab + _swap_halves(yn) * s_tab


def _qkv_body(x_ref, sh_ref, sc_ref, ng_ref, w_ref, cq_ref, sq_ref, ck_ref, sk_ref, q_ref, k_ref, v_ref):
    h = _modnorm(x_ref[...], ng_ref[...], sc_ref[0], sh_ref[0]).astype(BF16)
    zt = lax.dot_general(w_ref[...], h, (((1,), (1,)), ((), ())), preferred_element_type=F32)
    nq = N_HEADS * HEAD_DIM
    nkv = N_KV * HEAD_DIM
    cq, sq, ck, sk = cq_ref[...], sq_ref[...], ck_ref[...], sk_ref[...]
    for hd in range(N_HEADS):
        lo = hd * HEAD_DIM
        q_ref[lo:lo + HEAD_DIM, :] = _norm_rope_t(zt[lo:lo + HEAD_DIM], cq, sq).astype(BF16)
    kt = jnp.concatenate(
        [_norm_rope_t(zt[nq + kh * HEAD_DIM:nq + (kh + 1) * HEAD_DIM], ck, sk) for kh in range(N_KV)],
        axis=0)
    k_ref[...] = kt.T.astype(BF16)
    v_ref[...] = zt[nq + nkv:].astype(BF16)


def _rope_tables(q_g, k_g):
    half = HEAD_DIM // 2
    quarter = HEAD_DIM // 4
    pos = jnp.arange(SEQ)
    pos_row = (pos // GRID_W).astype(F32)
    pos_col = (pos % GRID_W).astype(F32)
    inv_freq = ROPE_THETA ** (-jnp.arange(0, half, 2, dtype=F32) / half)
    ang_row = inv_freq[:, None] * pos_row[None, :]
    ang_col = inv_freq[:, None] * pos_col[None, :]
    ang = jnp.concatenate([ang_row, ang_row, ang_col, ang_col], axis=0)
    cos = jnp.concatenate([jnp.cos(ang), jnp.ones((HEAD_DIM, T_QKV), F32)], axis=1)
    sin = jnp.concatenate([jnp.sin(ang), jnp.zeros((HEAD_DIM, T_QKV), F32)], axis=1)
    first = ((jnp.arange(HEAD_DIM) % half) < quarter)[:, None]
    sin = jnp.where(first, -sin, sin)

    def tables(g, scale):
        g = g.astype(F32) * scale
        partner = _swap_halves(g[:, None])
        return g[:, None] * cos, partner * sin

    return tables(q_g, HEAD_DIM ** -0.5 * LOG2_E) + tables(k_g, 1.0)


def _qkv(x, sh, sc, ng, w_qkv, q_g, k_g):
    n = x.shape[0]
    nt = n // T_QKV
    nkv = N_KV * HEAD_DIM
    wd = w_qkv.shape[1]
    per_seq = SEQ // T_QKV
    tab_spec = pl.BlockSpec(
        (HEAD_DIM, T_QKV), lambda i: (0, jnp.where(i < N_LAT // T_QKV, i % per_seq, per_seq)))
    mod = _mod_spec(T_QKV)
    return pl.pallas_call(
        _qkv_body,
        grid=(nt,),
        in_specs=[
            pl.BlockSpec((T_QKV, D), lambda i: (i, 0)),
            mod, mod,
            _const_spec((1, D)),
            _const_spec((wd, D)),
            tab_spec, tab_spec, tab_spec, tab_spec,
        ],
        out_specs=[pl.BlockSpec((D, T_QKV), lambda i: (0, i)),
                   pl.BlockSpec((T_QKV, nkv), lambda i: (i, 0)),
                   pl.BlockSpec((nkv, T_QKV), lambda i: (0, i))],
        out_shape=[jax.ShapeDtypeStruct((D, n), BF16),
                   jax.ShapeDtypeStruct((n, nkv), BF16),
                   jax.ShapeDtypeStruct((nkv, n), BF16)],
        compiler_params=pltpu.CompilerParams(
            dimension_semantics=("arbitrary",), vmem_limit_bytes=VMEM_BIG),
        name="attn_qkv",
    )(x, sh, sc, ng.reshape(1, D), w_qkv.T.astype(BF16), *_rope_tables(q_g, k_g))


def _attn_heads(qt_ref, k_all, vt_all, band, sink_ref, o_ref):
    group = N_HEADS // N_KV
    nk = k_all.shape[0]
    ones = jnp.ones((BF16_ROWS, nk), BF16)
    zeros = jnp.zeros((HEAD_DIM, group * T_Q), BF16)
    outs = []
    scores = []
    probs = []
    for kh in range(N_KV):
        q4 = jnp.concatenate(
            [qt_ref[(kh * group + g) * HEAD_DIM:(kh * group + g + 1) * HEAD_DIM, :] for g in range(group)],
            axis=1)
        qpad = jnp.concatenate([q4, zeros] if kh % 2 == 0 else [zeros, q4], axis=0)
        k2 = k_all[:, (kh // 2) * LANES:(kh // 2 + 1) * LANES]
        scores.append(jnp.dot(k2, qpad, preferred_element_type=F32))
    for kh in range(N_KV):
        st = scores[kh]
        blocks = [st[c * T_Q:(c + 1) * T_Q] for c in range(nk // T_Q)]
        if band is not None:
            blocks[0] = jnp.where(band[0], blocks[0], NEG_INF)
            blocks[2] = jnp.where(band[1], blocks[2], NEG_INF)
        best = blocks[0]
        for blk in blocks[1:]:
            best = jnp.maximum(best, blk)
        sink = jnp.concatenate(
            [jnp.full((1, T_Q), sink_ref[kh * group + g] * LOG2_E, F32) for g in range(group)], axis=1)
        m = jnp.maximum(jnp.max(best, axis=0, keepdims=True), sink)
        pt = jnp.concatenate([jnp.exp2(blk - m).astype(BF16) for blk in blocks], axis=0)
        probs.append((pt, jnp.exp2(sink - m)))
    for kh in range(N_KV):
        pt, sink_term = probs[kh]
        vt_aug = jnp.concatenate([vt_all[kh * HEAD_DIM:(kh + 1) * HEAD_DIM, :], ones], axis=0)
        ot = jnp.dot(vt_aug, pt, preferred_element_type=F32)
        den = ot[HEAD_DIM:HEAD_DIM + 1] + sink_term
        o = ot[:HEAD_DIM] / den
        outs.extend(o[:, g * T_Q:(g + 1) * T_Q] for g in range(group))
    o_ref[...] = jnp.concatenate(outs, axis=0).T.astype(BF16)


def _attn_body(sink_ref, q_ref, kp_ref, kc_ref, kn_ref, kx_ref, vp_ref, vc_ref, vn_ref, vx_ref, o_ref,
               *, n_q_blocks):
    iq = pl.program_id(1)
    per_seq = SEQ // T_Q
    group = N_HEADS // N_KV

    @pl.when(iq < per_seq)
    def _():
        k_all = jnp.concatenate([kp_ref[...], kc_ref[...], kn_ref[...], kx_ref[...]], axis=0)
        vt_all = jnp.concatenate([vp_ref[...], vc_ref[...], vn_ref[...], vx_ref[...]], axis=1)
        c = lax.broadcasted_iota(I32, (T_Q, group * T_Q), 0)
        r = lax.broadcasted_iota(I32, (T_Q, group * T_Q), 1) % T_Q
        far = 2 * T_Q
        mask_prev = c >= r + jnp.where(iq > 0, 0, far)
        mask_next = c <= r - jnp.where(iq < per_seq - 1, 0, far)
        _attn_heads(q_ref, k_all, vt_all, (mask_prev, mask_next), sink_ref, o_ref)

    if n_q_blocks > per_seq:
        @pl.when(iq >= per_seq)
        def _():
            _attn_heads(q_ref, kx_ref[...], vx_ref[...], None, sink_ref, o_ref)


def _attention(qt, k, vt, sink, need_ctx):
    assert T_Q == WINDOW
    per_seq = SEQ // T_Q
    ctx_blocks = CTX_LEN // T_Q
    n_q_blocks = per_seq + (ctx_blocks if need_ctx else 0)
    n_out = N_ALL if need_ctx else N_LAT
    lat_blocks = N_LAT // T_Q
    kw = k.shape[1]

    def q_blk(b, iq):
        return jnp.where(iq < per_seq, b * per_seq + iq, lat_blocks + b * ctx_blocks + (iq - per_seq))

    def win_blk(b, iq, off):
        return b * per_seq + jnp.clip(iq + off, 0, per_seq - 1)

    ctx_blk = lambda b: N_LAT // CTX_LEN + b
    k_win = lambda off: pl.BlockSpec((T_Q, kw), lambda b, iq, *_: (win_blk(b, iq, off), 0))
    v_win = lambda off: pl.BlockSpec((kw, T_Q), lambda b, iq, *_: (0, win_blk(b, iq, off)))
    k_ctx = pl.BlockSpec((CTX_LEN, kw), lambda b, iq, *_: (ctx_blk(b), 0))
    v_ctx = pl.BlockSpec((kw, CTX_LEN), lambda b, iq, *_: (0, ctx_blk(b)))
    grid_spec = pltpu.PrefetchScalarGridSpec(
        num_scalar_prefetch=1,
        grid=(NB, n_q_blocks),
        in_specs=[pl.BlockSpec((D, T_Q), lambda b, iq, *_: (0, q_blk(b, iq))),
                  k_win(-1), k_win(0), k_win(1), k_ctx, v_win(-1), v_win(0), v_win(1), v_ctx],
        out_specs=pl.BlockSpec((T_Q, D), lambda b, iq, *_: (q_blk(b, iq), 0)),
    )
    return pl.pallas_call(
        functools.partial(_attn_body, n_q_blocks=n_q_blocks),
        grid_spec=grid_spec,
        out_shape=jax.ShapeDtypeStruct((n_out, D), BF16),
        compiler_params=pltpu.CompilerParams(
            dimension_semantics=("arbitrary", "arbitrary"), vmem_limit_bytes=VMEM_BIG),
        name="attn_core",
    )(sink.astype(F32), qt, k, k, k, k, vt, vt, vt, vt)


def kernel(x, c, ctx, c_ctx, ada_w, ada_b, norm_mix_g, norm_ffn_g, ev_w_in, ev_ln_g, ev_ln_b, ev_ws,
           ev_bs, ev_conv_w, ev_conv_b, ev_cnorm_g, ev_w_out, od_w_qkv, od_q_g, od_k_g, od_sink, od_w_o,
           ff_w1, ff_w3, ff_w2, moe_router, moe_w1, moe_w3, moe_w2):
    assert x.shape == (NB, SEQ, D) and ctx.shape == (NB, CTX_LEN, D)
    mods = _ada_mods(c, c_ctx, ada_w, ada_b)
    xa = None
    ev_w_in_b, ev_w_out_b = ev_w_in.astype(BF16), ev_w_out.astype(BF16)
    for li in range(DEPTH):
        need_ctx = li < DEPTH - 1
        j = li // 2
        sh1, sc1, g1, sh2, sc2, g2 = mods[li]
        if li % 2 == 0:
            xin = (x.reshape(N_LAT, D), ctx.reshape(N_CTX, D)) if li == 0 else (xa, None)
            xa = _even_mixer(*xin, sh1, sc1, g1, norm_mix_g[li], ev_w_in_b, ev_ln_g[j], ev_ln_b[j],
                             ev_ws[j], ev_bs[j], ev_conv_w[j], ev_conv_b[j], ev_cnorm_g[j], ev_w_out_b, j)
            xa = _ffn_dense(xa, sh2, sc2, g2, norm_ffn_g[li], ff_w1, ff_w3, ff_w2, j)
        else:
            qt, k, vt = _qkv(xa, sh1, sc1, norm_mix_g[li], od_w_qkv[j], od_q_g[j], od_k_g[j])
            o = _attention(qt, k, vt, od_sink[j], need_ctx)
            xa = _attn_out_moe(o, od_w_o[j], g1, xa, sh2, sc2, g2, norm_ffn_g[li], moe_router[j],
                               moe_w1, moe_w3, moe_w2, j)
    return xa[:N_LAT].reshape(NB, SEQ, D)
```

```python
import functools

import jax
import jax.numpy as jnp
from jax import lax
from jax.experimental import pallas as pl
from jax.experimental.pallas import tpu as pltpu

F32 = jnp.float32
BF16 = jnp.bfloat16
I32 = jnp.int32

D = 1024
NB = 8
SEQ = 2048
CTX_LEN = 256
DEPTH = 4
GRID_W = 64
CHUNK = 128
A_GROUPS = 8
CONV_W = 31
N_HEADS = 16
N_KV = 4
HEAD_DIM = 64
WINDOW = 128
ROPE_THETA = 10000.0
D_FF = 3584
N_EXPERTS = 8
EPS = 1e-6
NEG_INF = -1e30
LOG2_E = 1.4426950408889634

N_LAT = NB * SEQ
N_CTX = NB * CTX_LEN
N_ALL = N_LAT + N_CTX
CTX_MOD_ROW = NB

LANES = 128
SUBLANES = 8
BF16_ROWS = 16

T_EVEN = 256
EVEN_PAIR = 2
HALO = BF16_ROWS
T_FFN = 1024
T_FF = 512
T_ROUTE = 512
T_MOE = 1024
MOE_PARTS = 4
SEG_ALIGN = SUBLANES
S_LOC = 2 * T_ROUTE + N_EXPERTS * SEG_ALIGN
SEG_SIZES = tuple(T_ROUTE >> s for s in range(T_ROUTE.bit_length()) if T_ROUTE >> s >= SEG_ALIGN)
assert SEG_SIZES[-1] == SEG_ALIGN
T_QKV = 512
T_Q = 128

VMEM_BIG = 52 * 1024 * 1024


def _mod_spec(tile):
    per = SEQ // tile
    return pl.BlockSpec((1, 1, D), lambda i, *_: (jnp.minimum(i // per, CTX_MOD_ROW), 0, 0))


def _const_spec(shape):
    nd = len(shape)
    return pl.BlockSpec(shape, lambda *_: (0,) * nd)


def _modnorm(x, g, sc, sh):
    ms = jnp.mean(x * x, axis=-1, keepdims=True)
    return (x * lax.rsqrt(ms + EPS) * g) * (1.0 + sc) + sh


def _silu(x):
    return x * jax.nn.sigmoid(x)


def _ada_body(c_ref, w_ref, b_ref, o_ref):
    a = _silu(c_ref[...])
    o_ref[0] = jnp.dot(a.astype(BF16), w_ref[0].astype(BF16), preferred_element_type=F32) + b_ref[0]


def _ada_mods(c, c_ctx, ada_w, ada_b):
    rows = BF16_ROWS
    cc = jnp.concatenate([c, c_ctx[None, :], jnp.zeros((rows - NB - 1, D), F32)], axis=0)
    out = pl.pallas_call(
        _ada_body,
        grid=(DEPTH, 6),
        in_specs=[
            pl.BlockSpec((rows, D), lambda l, n: (0, 0)),
            pl.BlockSpec((1, D, D), lambda l, n: (l, 0, n)),
            pl.BlockSpec((1, 1, D), lambda l, n: (l, 0, n)),
        ],
        out_specs=pl.BlockSpec((1, rows, D), lambda l, n: (l, 0, n)),
        out_shape=jax.ShapeDtypeStruct((DEPTH, rows, 6 * D), F32),
        name="ada_mod",
    )(cc, ada_w, ada_b.reshape(DEPTH, 1, 6 * D))
    m = out[:, :NB + 1].reshape(DEPTH, NB + 1, 6, 1, D)
    return [[m[l, :, k] for k in range(6)] for l in range(DEPTH)]


def _even_body(x_ref, xc_ref, xp_ref, xn_ref, sh_ref, sc_ref, gate_ref, ng_ref, win_ref, lng_ref, lnb_ref,
               ws_ref, bs_ref, cw_ref, cb_ref, cng_ref, wout_ref, o_ref, gext_ref, *, split):
    step = pl.program_id(0)
    per_seq = SEQ // T_EVEN
    ext = T_EVEN + 2 * HALO
    halves = range(EVEN_PAIR)
    x_all = x_ref[...]
    if split:
        x_all = jnp.where(step < N_LAT // (EVEN_PAIR * T_EVEN), x_all, xc_ref[...])
    tiles = [x_all[hf * T_EVEN:(hf + 1) * T_EVEN] for hf in halves]
    befores = [xp_ref[...]] + [tiles[hf - 1][T_EVEN - HALO:] for hf in halves[1:]]
    afters = [tiles[hf + 1][:HALO] for hf in halves[:-1]] + [xn_ref[...]]

    hs = [_modnorm(jnp.concatenate([befores[hf], tiles[hf], afters[hf]], axis=0),
                   ng_ref[...], sc_ref[0], sh_ref[0]).astype(BF16) for hf in halves]
    zbs = [jnp.dot(hs[hf], win_ref[0, :, 2 * D:], preferred_element_type=F32) for hf in halves]
    zas = [jnp.dot(hs[hf][HALO:HALO + T_EVEN], win_ref[0, :, :2 * D], preferred_element_type=F32)
           for hf in halves]

    for hf in halves:
        i = step * EVEN_PAIR + hf
        is_lat = i < N_LAT // T_EVEN
        pos = i % per_seq
        is_start = jnp.logical_or(jnp.logical_not(is_lat), pos == 0)
        is_end = jnp.logical_or(jnp.logical_not(is_lat), pos == per_seq - 1)
        gg = zbs[hf][:, :D] * jax.nn.sigmoid(zbs[hf][:, D:])
        row = lax.broadcasted_iota(I32, (ext, 1), 0)
        lo = jnp.where(is_start, HALO, 0)
        hi = jnp.where(is_end, T_EVEN + HALO, ext)
        gg = jnp.where(jnp.logical_and(row >= lo, row < hi), gg, 0.0)
        for cbk in range(D // LANES):
            gext_ref[hf, cbk] = gg[:, cbk * LANES:(cbk + 1) * LANES]

    rows_per = 64
    gdim = D // A_GROUPS
    for hf in halves:
        cols = []
        for cbk in range(D // LANES):
            blocks = []
            for rb in range(T_EVEN // rows_per):
                acc = jnp.zeros((rows_per, LANES), F32)
                for k in range(CONV_W):
                    off = rb * rows_per + k + HALO - CONV_W // 2
                    acc = acc + cw_ref[cbk, pl.ds(k, 1), :] * gext_ref[hf, cbk, pl.ds(off, rows_per), :]
                blocks.append(acc)
            cols.append(jnp.concatenate(blocks, axis=0))

        cv = jnp.concatenate(cols, axis=1) + cb_ref[...]
        ms = jnp.mean(cv * cv, axis=-1, keepdims=True)
        yb = _silu(cv * lax.rsqrt(ms + EPS) * cng_ref[...])

        u = jax.nn.gelu(zas[hf][:, :D])
        v = jax.nn.gelu(zas[hf][:, D:])
        mu = jnp.mean(v, axis=-1, keepdims=True)
        vc = v - mu
        var = jnp.mean(vc * vc, axis=-1, keepdims=True)
        vn = (vc * lax.rsqrt(var + EPS) * lng_ref[...] + lnb_ref[...]).astype(BF16)
        chunks = []
        for ck in range(T_EVEN // CHUNK):
            blocks = [
                jnp.dot(ws_ref[g], vn[ck * CHUNK:(ck + 1) * CHUNK, g * gdim:(g + 1) * gdim],
                        preferred_element_type=F32)
                for g in range(A_GROUPS)
            ]
            chunks.append(jnp.concatenate(blocks, axis=1) + bs_ref[...])
        ya = u * jnp.concatenate(chunks, axis=0)

        y = (jnp.dot(ya.astype(BF16), wout_ref[0, :D], preferred_element_type=F32)
             + jnp.dot(yb.astype(BF16), wout_ref[0, D:], preferred_element_type=F32))
        o_ref[hf * T_EVEN:(hf + 1) * T_EVEN, :] = tiles[hf] + gate_ref[0] * y


def _even_mixer(x, x_ctx, sh, sc, gate, ng, w_in, ln_g, ln_b, ws, bs, conv_w, conv_b, cn_g, w_out, j):
    split = x_ctx is not None
    n = x.shape[0] + (x_ctx.shape[0] if split else 0)
    rows = EVEN_PAIR * T_EVEN
    nt = n // rows
    hb = rows // HALO
    last = x.shape[0] // HALO - 1
    x_steps = x.shape[0] // rows
    if split:
        main_specs = [pl.BlockSpec((rows, D), lambda i: (jnp.minimum(i, x_steps - 1), 0)),
                      pl.BlockSpec((rows, D), lambda i: (jnp.maximum(i - x_steps, 0), 0))]
        x_second = x_ctx
    else:
        main_specs = [pl.BlockSpec((rows, D), lambda i: (i, 0)), _const_spec((HALO, D))]
        x_second = x[:HALO]
    ncb = D // LANES
    bs_full = jnp.repeat(bs.T, D // A_GROUPS, axis=1)
    taps = -(-CONV_W // SUBLANES) * SUBLANES
    cw = jnp.pad(conv_w, ((0, taps - CONV_W), (0, 0))).reshape(taps, ncb, LANES).transpose(1, 0, 2)
    mod = _mod_spec(rows)
    row1 = lambda a: a.reshape(1, D)
    return pl.pallas_call(
        functools.partial(_even_body, split=split),
        grid=(nt,),
        in_specs=main_specs + [
            pl.BlockSpec((HALO, D), lambda i: (jnp.clip(i * hb - 1, 0, last), 0)),
            pl.BlockSpec((HALO, D), lambda i: (jnp.minimum((i + 1) * hb, last), 0)),
            mod, mod, mod,
            _const_spec((1, D)),
            pl.BlockSpec((1, D, 4 * D), lambda i: (j, 0, 0)),
            _const_spec((1, D)), _const_spec((1, D)),
            _const_spec((A_GROUPS, CHUNK, CHUNK)),
            _const_spec((CHUNK, D)),
            _const_spec((ncb, taps, LANES)),
            _const_spec((1, D)), _const_spec((1, D)),
            pl.BlockSpec((1, 2 * D, D), lambda i: (j, 0, 0)),
        ],
        out_specs=pl.BlockSpec((rows, D), lambda i: (i, 0)),
        out_shape=jax.ShapeDtypeStruct((n, D), F32),
        scratch_shapes=[
            pltpu.VMEM((EVEN_PAIR, ncb, T_EVEN + 2 * HALO, LANES), F32),
        ],
        compiler_params=pltpu.CompilerParams(
            dimension_semantics=("arbitrary",), vmem_limit_bytes=VMEM_BIG),
        name="even_mixer",
    )(x, x_second, x, x, sh, sc, gate, row1(ng), w_in, row1(ln_g), row1(ln_b), ws.astype(BF16),
      bs_full, cw, row1(conv_b), row1(cn_g), w_out)


def _ffn_dense_body(x_ref, sh_ref, sc_ref, gate_ref, ng_ref, w1_ref, w3_ref, w2_ref, o_ref,
                    h_ref, acc_ref):
    j = pl.program_id(1)

    @pl.when(jnp.logical_and(pl.program_id(0) == 0, j == 0))
    def _():
        acc_ref[...] = jnp.zeros_like(acc_ref)

    @pl.when(j == 0)
    def _():
        h_ref[...] = _modnorm(x_ref[...], ng_ref[...], sc_ref[0], sh_ref[0]).astype(BF16)

    h = h_ref[...]
    a = jnp.dot(h, w1_ref[0].astype(BF16), preferred_element_type=F32)
    b = jnp.dot(h, w3_ref[0].astype(BF16), preferred_element_type=F32)
    t = (_silu(a) * b).astype(BF16)
    prev = jnp.where(j == 0, 0.0, acc_ref[...])
    acc_ref[...] = prev + jnp.dot(t, w2_ref[0].astype(BF16), preferred_element_type=F32)

    @pl.when(j == pl.num_programs(1) - 1)
    def _():
        o_ref[...] = x_ref[...] + gate_ref[0] * acc_ref[...]


def _ffn_dense(x, sh, sc, gate, ng, w1, w3, w2, layer):
    n = x.shape[0]
    mod = _mod_spec(T_FFN)
    return pl.pallas_call(
        _ffn_dense_body,
        grid=(n // T_FFN, D_FF // T_FF),
        in_specs=[
            pl.BlockSpec((T_FFN, D), lambda i, j: (i, 0)),
            mod, mod, mod,
            _const_spec((1, D)),
            pl.BlockSpec((1, D, T_FF), lambda i, j: (layer, 0, j)),
            pl.BlockSpec((1, D, T_FF), lambda i, j: (layer, 0, j)),
            pl.BlockSpec((1, T_FF, D), lambda i, j: (layer, j, 0)),
        ],
        out_specs=pl.BlockSpec((T_FFN, D), lambda i, j: (i, 0)),
        out_shape=jax.ShapeDtypeStruct((n, D), F32),
        scratch_shapes=[pltpu.VMEM((T_FFN, D), BF16), pltpu.VMEM((T_FFN, D), F32)],
        compiler_params=pltpu.CompilerParams(
            dimension_semantics=("arbitrary", "arbitrary"), vmem_limit_bytes=VMEM_BIG),
        name="ffn_dense",
    )(x, sh, sc, gate, ng.reshape(1, D), w1, w3, w2)


def _router_body(a_ref, wo_ref, g1_ref, x_ref, sh_ref, sc_ref, ng_ref, rt_ref,
                 xo_ref, hb_ref, s1_ref, s2_ref, w1_ref, w2_ref, pc_ref):
    xn = x_ref[...] + g1_ref[0] * jnp.dot(a_ref[...], wo_ref[...], preferred_element_type=F32)
    xo_ref[...] = xn
    h = _modnorm(xn, ng_ref[...], sc_ref[0], sh_ref[0])
    hb_ref[...] = h.astype(BF16)
    lg = lax.dot_general(rt_ref[...], h, (((1,), (1,)), ((), ())),
                         precision=lax.Precision.HIGHEST, preferred_element_type=F32)
    e = lax.broadcasted_iota(I32, lg.shape, 0).astype(F32)
    big = float(N_EXPERTS)
    m1 = jnp.max(lg, axis=0, keepdims=True)
    i1 = jnp.min(jnp.where(lg == m1, e, big), axis=0, keepdims=True)
    lg2 = jnp.where(e == i1, -jnp.inf, lg)
    m2 = jnp.max(lg2, axis=0, keepdims=True)
    i2 = jnp.min(jnp.where(lg2 == m2, e, big), axis=0, keepdims=True)
    e2 = jnp.exp(m2 - m1)
    den = 1.0 + e2
    w1_ref[0] = 1.0 / den
    w2_ref[0] = e2 / den

    sel1 = e == i1
    sel2 = e == i2
    member = jnp.where(jnp.logical_or(sel1, sel2), 1.0, 0.0)
    t = lg.shape[1]
    before = (lax.broadcasted_iota(I32, (t, t), 0) < lax.broadcasted_iota(I32, (t, t), 1))
    tri = jnp.where(before, 1.0, 0.0).astype(BF16)
    rank = jnp.dot(member.astype(BF16), tri, preferred_element_type=F32)
    count = jnp.sum(member, axis=1, keepdims=True)
    padded = jnp.ceil(count * (1.0 / SEG_ALIGN)) * SEG_ALIGN
    e_col = lax.broadcasted_iota(I32, (N_EXPERTS, 1), 0)
    seg_start = jnp.zeros((N_EXPERTS, 1), F32)
    for ex in range(1, N_EXPERTS):
        below = jnp.sum(jnp.where(e_col < ex, padded, 0.0), axis=0, keepdims=True)
        seg_start = jnp.where(e_col == ex, below, seg_start)
    slot = rank + seg_start
    s1 = jnp.sum(jnp.where(sel1, slot, 0.0), axis=0, keepdims=True).astype(I32)
    s2 = jnp.sum(jnp.where(sel2, slot, 0.0), axis=0, keepdims=True).astype(I32)
    s1_ref[0] = s1
    s2_ref[0] = s2
    pc_ref[0] = jnp.broadcast_to(padded.astype(I32), (N_EXPERTS, LANES))


def _router(a, w_o, gate1, x, sh, sc, ng, router):
    n = a.shape[0]
    nt = n // T_ROUTE
    mod = _mod_spec(T_ROUTE)
    row_spec = pl.BlockSpec((T_ROUTE, D), lambda i: (i, 0))
    vec_spec = pl.BlockSpec((1, 1, T_ROUTE), lambda i: (i, 0, 0))
    vec_i = jax.ShapeDtypeStruct((nt, 1, T_ROUTE), I32)
    vec_f = jax.ShapeDtypeStruct((nt, 1, T_ROUTE), F32)
    return pl.pallas_call(
        _router_body,
        grid=(nt,),
        in_specs=[
            row_spec,
            _const_spec((D, D)),
            mod,
            row_spec,
            mod, mod,
            _const_spec((1, D)),
            _const_spec((N_EXPERTS, D)),
        ],
        out_specs=[row_spec, row_spec, vec_spec, vec_spec, vec_spec, vec_spec,
                   pl.BlockSpec((1, N_EXPERTS, LANES), lambda i: (i, 0, 0))],
        out_shape=[jax.ShapeDtypeStruct((n, D), F32), jax.ShapeDtypeStruct((n, D), BF16),
                   vec_i, vec_i, vec_f, vec_f, jax.ShapeDtypeStruct((nt, N_EXPERTS, LANES), I32)],
        compiler_params=pltpu.CompilerParams(
            dimension_semantics=("arbitrary",), vmem_limit_bytes=VMEM_BIG),
        name="moe_router",
    )(a, w_o.astype(BF16), gate1, x, sh, sc, ng.reshape(1, D), router.T)


def _dispatch_body(start_ref, len_ref, tail_start_ref, tail_len_ref, nv_ref,
                   h_ref, s1_ref, s2_ref, xs_ref, lbuf_ref, zbuf_ref, sem,
                   *, n_tiles, min_tiles):
    i = pl.program_id(0)
    slot = i % 2
    other = 1 - slot
    h = h_ref[...]
    rows = lax.broadcasted_iota(I32, (S_LOC, T_ROUTE), 0)
    perm = jnp.where(jnp.logical_or(rows == s1_ref[0], rows == s2_ref[0]), 1.0, 0.0).astype(BF16)
    lbuf_ref[slot] = jnp.dot(perm, h, preferred_element_type=F32)

    xs_at = lambda a, p: xs_ref.at[pl.ds(a, p)]
    _segment_copies(_tile_plan(i, len_ref, start_ref), lambda a, p: lbuf_ref.at[slot, pl.ds(a, p)], xs_at,
                    sem.at[slot], start=True)
    _segment_copies(_tile_plan(jnp.maximum(i - 1, 0), len_ref, start_ref),
                    lambda a, p: lbuf_ref.at[other, pl.ds(a, p)], xs_at, sem.at[other], start=False,
                    enable=i > 0)

    @pl.when(i == pl.num_programs(0) - 1)
    def _():
        _segment_copies(_tile_plan(i, len_ref, start_ref), lambda a, p: lbuf_ref.at[slot, pl.ds(a, p)],
                        xs_at, sem.at[slot], start=False)
        zbuf_ref[...] = jnp.zeros_like(zbuf_ref)
        tails = [(tail_len_ref[ex], 0, tail_start_ref[ex]) for ex in range(N_EXPERTS)]
        zero_at = lambda a, p: zbuf_ref.at[pl.ds(a, p)]
        _segment_copies(tails, zero_at, xs_at, sem.at[slot], start=True)
        _segment_copies(tails, zero_at, xs_at, sem.at[slot], start=False)
        for tile in range(min_tiles, n_tiles):
            @pl.when(tile >= nv_ref[0])
            def _(tile=tile):
                cp = pltpu.make_async_copy(zbuf_ref, xs_ref.at[pl.ds(tile * T_MOE, T_MOE)], sem.at[slot])
                cp.start()
                cp.wait()


def _dispatch(h, s1, s2, seg_start, seg_len, tail_start, tail_len, n_valid, n_tiles):
    n = h.shape[0]
    nt = n // T_ROUTE
    vec_spec = pl.BlockSpec((1, 1, T_ROUTE), lambda i, *_: (i, 0, 0))
    grid_spec = pltpu.PrefetchScalarGridSpec(
        num_scalar_prefetch=5,
        grid=(nt,),
        in_specs=[pl.BlockSpec((T_ROUTE, D), lambda i, *_: (i, 0)),
                  vec_spec, vec_spec],
        out_specs=pl.BlockSpec(memory_space=pl.ANY),
        scratch_shapes=[pltpu.VMEM((2, S_LOC, D), F32), pltpu.VMEM((T_MOE, D), F32),
                        pltpu.SemaphoreType.DMA((2,))],
    )
    return pl.pallas_call(
        functools.partial(_dispatch_body, n_tiles=n_tiles, min_tiles=(2 * n) // T_MOE),
        grid_spec=grid_spec,
        out_shape=jax.ShapeDtypeStruct((n_tiles * T_MOE, D), F32),
        compiler_params=pltpu.CompilerParams(
            dimension_semantics=("arbitrary",), vmem_limit_bytes=VMEM_BIG),
        name="moe_dispatch",
    )(seg_start.reshape(-1), seg_len.reshape(-1), tail_start, tail_len, n_valid, h, s1, s2)


def _tile_plan(tile, len_ref, far_ref):
    near = 0
    plan = []
    for ex in range(N_EXPERTS):
        ln = len_ref[tile * N_EXPERTS + ex]
        plan.append((ln, near, far_ref[tile * N_EXPERTS + ex]))
        near = near + ln
    return plan


def _segment_copies(plan, src_at, dst_at, sem, start, enable=True):
    for ln, src, dst in plan:
        for p in SEG_SIZES:
            off = jnp.bitwise_and(ln, -(2 * p))

            @pl.when(jnp.logical_and(jnp.bitwise_and(ln, p) != 0, enable))
            def _(off=off, src=src, dst=dst, p=p):
                cp = pltpu.make_async_copy(src_at(pl.multiple_of(src + off, SEG_ALIGN), p),
                                           dst_at(pl.multiple_of(dst + off, SEG_ALIGN), p), sem)
                if start:
                    cp.start()
                else:
                    cp.wait()


def _ffn_moe_body(te_ref, parts_ref, nv_ref, xs_ref, w1_ref, w3_ref, w2_ref, o_ref, acc_ref):
    del te_ref
    i = pl.program_id(0)
    j = pl.program_id(1)
    valid = i < nv_ref[0]

    @pl.when(jnp.logical_and(i == 0, j == 0))
    def _():
        acc_ref[...] = jnp.zeros_like(acc_ref)

    for parts in range(1, MOE_PARTS + 1):
        @pl.when(parts_ref[i] == parts)
        def _(rows=parts * (T_MOE // MOE_PARTS)):
            h = xs_ref[:rows].astype(BF16)
            a = jnp.dot(h, w1_ref[0, 0].astype(BF16), preferred_element_type=F32)
            b = jnp.dot(h, w3_ref[0, 0].astype(BF16), preferred_element_type=F32)
            t = (_silu(a) * b).astype(BF16)
            prev = jnp.where(j == 0, 0.0, acc_ref[:rows])
            acc_ref[:rows] = prev + jnp.dot(t, w2_ref[0, 0].astype(BF16), preferred_element_type=F32)
            if rows < T_MOE:
                @pl.when(j == 0)
                def _():
                    acc_ref[rows:] = jnp.zeros((T_MOE - rows, D), F32)

    last = j == pl.num_programs(1) - 1

    @pl.when(jnp.logical_and(valid, last))
    def _():
        o_ref[...] = acc_ref[...]

    @pl.when(jnp.logical_and(jnp.logical_not(valid), last))
    def _():
        o_ref[...] = jnp.zeros_like(o_ref)


def _ffn_moe(xs, tile_expert, tile_parts, n_valid, w1, w3, w2, layer):
    nj = D_FF // T_FF

    def jj(i, j, nv):
        return jnp.where(i < nv[0], j, nj - 1)

    grid_spec = pltpu.PrefetchScalarGridSpec(
        num_scalar_prefetch=3,
        grid=(xs.shape[0] // T_MOE, nj),
        in_specs=[
            pl.BlockSpec((T_MOE, D), lambda i, j, te, tp, nv: (i, 0)),
            pl.BlockSpec((1, 1, D, T_FF), lambda i, j, te, tp, nv: (layer, te[i], 0, jj(i, j, nv))),
            pl.BlockSpec((1, 1, D, T_FF), lambda i, j, te, tp, nv: (layer, te[i], 0, jj(i, j, nv))),
            pl.BlockSpec((1, 1, T_FF, D), lambda i, j, te, tp, nv: (layer, te[i], jj(i, j, nv), 0)),
        ],
        out_specs=pl.BlockSpec((T_MOE, D), lambda i, j, te, tp, nv: (i, 0)),
        scratch_shapes=[pltpu.VMEM((T_MOE, D), F32)],
    )
    return pl.pallas_call(
        _ffn_moe_body,
        grid_spec=grid_spec,
        out_shape=jax.ShapeDtypeStruct(xs.shape, F32),
        compiler_params=pltpu.CompilerParams(
            dimension_semantics=("arbitrary", "arbitrary"), vmem_limit_bytes=VMEM_BIG),
        name="ffn_moe",
    )(tile_expert, tile_parts, n_valid, xs, w1, w3, w2)


def _combine_body(start_ref, len_ref, s1_ref, s2_ref, w1_ref, w2_ref, x_ref, gate_ref, os_ref, o_ref,
                  lo_ref, sem):
    i = pl.program_id(0)
    last = pl.num_programs(0) - 1
    slot = i % 2
    other = 1 - slot
    os_at = lambda a, p: os_ref.at[pl.ds(a, p)]

    def gather(tile, buf, start, enable=True):
        plan = [(ln, far, near) for ln, near, far in _tile_plan(tile, len_ref, start_ref)]
        _segment_copies(plan, os_at, lambda a, p: lo_ref.at[buf, pl.ds(a, p)], sem.at[buf], start, enable)

    @pl.when(i == 0)
    def _():
        lo_ref[...] = jnp.zeros_like(lo_ref)
        gather(i, slot, start=True)

    gather(jnp.minimum(i + 1, last), other, start=True, enable=i < last)
    gather(i, slot, start=False)

    lo = lo_ref[slot].astype(BF16)
    lane = lax.broadcasted_iota(I32, (T_ROUTE, S_LOC), 1)
    pick1 = jnp.where(lane == s1_ref[...], 1.0, 0.0).astype(BF16)
    pick2 = jnp.where(lane == s2_ref[...], 1.0, 0.0).astype(BF16)
    y = (w1_ref[...] * jnp.dot(pick1, lo, preferred_element_type=F32)
         + w2_ref[...] * jnp.dot(pick2, lo, preferred_element_type=F32))
    o_ref[...] = x_ref[...] + gate_ref[0] * y


def _combine(x, n, gate, seg_start, seg_len, s1, s2, w1, w2, o_sorted):
    nt = n // T_ROUTE
    col = lambda a: a.reshape(n, 1)
    col_spec = pl.BlockSpec((T_ROUTE, 1), lambda i, *_: (i, 0))
    grid_spec = pltpu.PrefetchScalarGridSpec(
        num_scalar_prefetch=2,
        grid=(nt,),
        in_specs=[col_spec, col_spec, col_spec, col_spec,
                  pl.BlockSpec((T_ROUTE, D), lambda i, *_: (i, 0)),
                  _mod_spec(T_ROUTE),
                  pl.BlockSpec(memory_space=pl.ANY)],
        out_specs=pl.BlockSpec((T_ROUTE, D), lambda i, *_: (i, 0)),
        scratch_shapes=[pltpu.VMEM((2, S_LOC, D), F32), pltpu.SemaphoreType.DMA((2,))],
    )
    return pl.pallas_call(
        _combine_body,
        grid_spec=grid_spec,
        out_shape=jax.ShapeDtypeStruct((n, D), F32),
        compiler_params=pltpu.CompilerParams(
            dimension_semantics=("arbitrary",), vmem_limit_bytes=VMEM_BIG),
        name="moe_combine",
    )(seg_start.reshape(-1), seg_len.reshape(-1), col(s1), col(s2), col(w1), col(w2), x, gate, o_sorted)


def _attn_out_moe(a, w_o, gate1, x, sh, sc, gate, ng, router, w1, w3, w2, layer):
    n = a.shape[0]
    nt = n // T_ROUTE
    x, h, s1, s2, p1, p2, pc = _router(a, w_o, gate1, x, sh, sc, ng, router)
    seg_len = pc[:, :, 0]
    total = jnp.sum(seg_len, axis=0)
    tiles_per = (total + T_MOE - 1) // T_MOE
    tile_ends = jnp.cumsum(tiles_per)
    group_start = (tile_ends - tiles_per) * T_MOE
    seg_start = group_start[None, :] + jnp.cumsum(seg_len, axis=0) - seg_len
    tail_start = group_start + total
    tail_len = tiles_per * T_MOE - total
    n_valid = tile_ends[-1:]
    n_tiles = -(-(2 * n + nt * N_EXPERTS * (SEG_ALIGN - 1)) // T_MOE) + N_EXPERTS
    tile_all = jnp.arange(n_tiles, dtype=I32)
    tile_ids = jnp.minimum(tile_all, n_valid - 1)
    tile_expert = jnp.sum((tile_ids[:, None] >= tile_ends[None, :]).astype(I32), axis=1)
    tile_rows = jnp.clip((group_start + total)[tile_expert] - tile_ids * T_MOE, 0, T_MOE)
    part = T_MOE // MOE_PARTS
    tile_parts = jnp.where(tile_all < n_valid, (tile_rows + part - 1) // part, 0).astype(I32)
    xs = _dispatch(h, s1, s2, seg_start, seg_len, tail_start, tail_len, n_valid, n_tiles)
    o_sorted = _ffn_moe(xs, tile_expert, tile_parts, n_valid, w1, w3, w2, layer)
    return _combine(x, n, gate, seg_start, seg_len, s1, s2, p1, p2, o_sorted)


def _swap_halves(y):
    q = HEAD_DIM // 4
    return jnp.concatenate([y[q:2 * q], y[:q], y[3 * q:], y[2 * q:3 * q]], axis=0)


def _norm_rope_t(zh, c_tab, s_tab):
    ss = jnp.mean(zh * zh, axis=0, keepdims=True)
    yn = zh * lax.rsqrt(ss + EPS)
    return yn * c_tab + _swap_halves(yn) * s_tab


def _qkv_body(x_ref, sh_ref, sc_ref, ng_ref, w_ref, cq_ref, sq_ref, ck_ref, sk_ref, q_ref, k_ref, v_ref):
    h = _modnorm(x_ref[...], ng_ref[...], sc_ref[0], sh_ref[0]).astype(BF16)
    zt = lax.dot_general(w_ref[...], h, (((1,), (1,)), ((), ())), preferred_element_type=F32)
    nq = N_HEADS * HEAD_DIM
    nkv = N_KV * HEAD_DIM
    cq, sq, ck, sk = cq_ref[...], sq_ref[...], ck_ref[...], sk_ref[...]
    for hd in range(N_HEADS):
        lo = hd * HEAD_DIM
        q_ref[lo:lo + HEAD_DIM, :] = _norm_rope_t(zt[lo:lo + HEAD_DIM], cq, sq).astype(BF16)
    kt = jnp.concatenate(
        [_norm_rope_t(zt[nq + kh * HEAD_DIM:nq + (kh + 1) * HEAD_DIM], ck, sk) for kh in range(N_KV)],
        axis=0)
    k_ref[...] = kt.T.astype(BF16)
    v_ref[...] = zt[nq + nkv:].astype(BF16)


def _rope_tables(q_g, k_g):
    half = HEAD_DIM // 2
    quarter = HEAD_DIM // 4
    pos = jnp.arange(SEQ)
    pos_row = (pos // GRID_W).astype(F32)
    pos_col = (pos % GRID_W).astype(F32)
    inv_freq = ROPE_THETA ** (-jnp.arange(0, half, 2, dtype=F32) / half)
    ang_row = inv_freq[:, None] * pos_row[None, :]
    ang_col = inv_freq[:, None] * pos_col[None, :]
    ang = jnp.concatenate([ang_row, ang_row, ang_col, ang_col], axis=0)
    cos = jnp.concatenate([jnp.cos(ang), jnp.ones((HEAD_DIM, T_QKV), F32)], axis=1)
    sin = jnp.concatenate([jnp.sin(ang), jnp.zeros((HEAD_DIM, T_QKV), F32)], axis=1)
    first = ((jnp.arange(HEAD_DIM) % half) < quarter)[:, None]
    sin = jnp.where(first, -sin, sin)

    def tables(g, scale):
        g = g.astype(F32) * scale
        partner = _swap_halves(g[:, None])
        return g[:, None] * cos, partner * sin

    return tables(q_g, HEAD_DIM ** -0.5 * LOG2_E) + tables(k_g, 1.0)


def _qkv(x, sh, sc, ng, w_qkv, q_g, k_g):
    n = x.shape[0]
    nt = n // T_QKV
    nkv = N_KV * HEAD_DIM
    wd = w_qkv.shape[1]
    per_seq = SEQ // T_QKV
    tab_spec = pl.BlockSpec(
        (HEAD_DIM, T_QKV), lambda i: (0, jnp.where(i < N_LAT // T_QKV, i % per_seq, per_seq)))
    mod = _mod_spec(T_QKV)
    return pl.pallas_call(
        _qkv_body,
        grid=(nt,),
        in_specs=[
            pl.BlockSpec((T_QKV, D), lambda i: (i, 0)),
            mod, mod,
            _const_spec((1, D)),
            _const_spec((wd, D)),
            tab_spec, tab_spec, tab_spec, tab_spec,
        ],
        out_specs=[pl.BlockSpec((D, T_QKV), lambda i: (0, i)),
                   pl.BlockSpec((T_QKV, nkv), lambda i: (i, 0)),
                   pl.BlockSpec((nkv, T_QKV), lambda i: (0, i))],
        out_shape=[jax.ShapeDtypeStruct((D, n), BF16),
                   jax.ShapeDtypeStruct((n, nkv), BF16),
                   jax.ShapeDtypeStruct((nkv, n), BF16)],
        compiler_params=pltpu.CompilerParams(
            dimension_semantics=("arbitrary",), vmem_limit_bytes=VMEM_BIG),
        name="attn_qkv",
    )(x, sh, sc, ng.reshape(1, D), w_qkv.T.astype(BF16), *_rope_tables(q_g, k_g))


def _attn_heads(qt_ref, k_all, vt_all, band, sink_ref, o_ref):
    group = N_HEADS // N_KV
    nk = k_all.shape[0]
    ones = jnp.ones((BF16_ROWS, nk), BF16)
    zeros = jnp.zeros((HEAD_DIM, group * T_Q), BF16)
    outs = []
    scores = []
    probs = []
    for kh in range(N_KV):
        q4 = jnp.concatenate(
            [qt_ref[(kh * group + g) * HEAD_DIM:(kh * group + g + 1) * HEAD_DIM, :] for g in range(group)],
            axis=1)
        qpad = jnp.concatenate([q4, zeros] if kh % 2 == 0 else [zeros, q4], axis=0)
        k2 = k_all[:, (kh // 2) * LANES:(kh // 2 + 1) * LANES]
        scores.append(jnp.dot(k2, qpad, preferred_element_type=F32))
    for kh in range(N_KV):
        st = scores[kh]
        blocks = [st[c * T_Q:(c + 1) * T_Q] for c in range(nk // T_Q)]
        if band is not None:
            blocks[0] = jnp.where(band[0], blocks[0], NEG_INF)
            blocks[2] = jnp.where(band[1], blocks[2], NEG_INF)
        best = blocks[0]
        for blk in blocks[1:]:
            best = jnp.maximum(best, blk)
        sink = jnp.concatenate(
            [jnp.full((1, T_Q), sink_ref[kh * group + g] * LOG2_E, F32) for g in range(group)], axis=1)
        m = jnp.maximum(jnp.max(best, axis=0, keepdims=True), sink)
        pt = jnp.concatenate([jnp.exp2(blk - m).astype(BF16) for blk in blocks], axis=0)
        probs.append((pt, jnp.exp2(sink - m)))
    for kh in range(N_KV):
        pt, sink_term = probs[kh]
        vt_aug = jnp.concatenate([vt_all[kh * HEAD_DIM:(kh + 1) * HEAD_DIM, :], ones], axis=0)
        ot = jnp.dot(vt_aug, pt, preferred_element_type=F32)
        den = ot[HEAD_DIM:HEAD_DIM + 1] + sink_term
        o = ot[:HEAD_DIM] / den
        outs.extend(o[:, g * T_Q:(g + 1) * T_Q] for g in range(group))
    o_ref[...] = jnp.concatenate(outs, axis=0).T.astype(BF16)


def _attn_body(sink_ref, q_ref, kp_ref, kc_ref, kn_ref, kx_ref, vp_ref, vc_ref, vn_ref, vx_ref, o_ref,
               *, n_q_blocks):
    iq = pl.program_id(1)
    per_seq = SEQ // T_Q
    group = N_HEADS // N_KV

    @pl.when(iq < per_seq)
    def _():
        k_all = jnp.concatenate([kp_ref[...], kc_ref[...], kn_ref[...], kx_ref[...]], axis=0)
        vt_all = jnp.concatenate([vp_ref[...], vc_ref[...], vn_ref[...], vx_ref[...]], axis=1)
        c = lax.broadcasted_iota(I32, (T_Q, group * T_Q), 0)
        r = lax.broadcasted_iota(I32, (T_Q, group * T_Q), 1) % T_Q
        far = 2 * T_Q
        mask_prev = c >= r + jnp.where(iq > 0, 0, far)
        mask_next = c <= r - jnp.where(iq < per_seq - 1, 0, far)
        _attn_heads(q_ref, k_all, vt_all, (mask_prev, mask_next), sink_ref, o_ref)

    if n_q_blocks > per_seq:
        @pl.when(iq >= per_seq)
        def _():
            _attn_heads(q_ref, kx_ref[...], vx_ref[...], None, sink_ref, o_ref)


def _attention(qt, k, vt, sink, need_ctx):
    assert T_Q == WINDOW
    per_seq = SEQ // T_Q
    ctx_blocks = CTX_LEN // T_Q
    n_q_blocks = per_seq + (ctx_blocks if need_ctx else 0)
    n_out = N_ALL if need_ctx else N_LAT
    lat_blocks = N_LAT // T_Q
    kw = k.shape[1]

    def q_blk(b, iq):
        return jnp.where(iq < per_seq, b * per_seq + iq, lat_blocks + b * ctx_blocks + (iq - per_seq))

    def win_blk(b, iq, off):
        return b * per_seq + jnp.clip(iq + off, 0, per_seq - 1)

    ctx_blk = lambda b: N_LAT // CTX_LEN + b
    k_win = lambda off: pl.BlockSpec((T_Q, kw), lambda b, iq, *_: (win_blk(b, iq, off), 0))
    v_win = lambda off: pl.BlockSpec((kw, T_Q), lambda b, iq, *_: (0, win_blk(b, iq, off)))
    k_ctx = pl.BlockSpec((CTX_LEN, kw), lambda b, iq, *_: (ctx_blk(b), 0))
    v_ctx = pl.BlockSpec((kw, CTX_LEN), lambda b, iq, *_: (0, ctx_blk(b)))
    grid_spec = pltpu.PrefetchScalarGridSpec(
        num_scalar_prefetch=1,
        grid=(NB, n_q_blocks),
        in_specs=[pl.BlockSpec((D, T_Q), lambda b, iq, *_: (0, q_blk(b, iq))),
                  k_win(-1), k_win(0), k_win(1), k_ctx, v_win(-1), v_win(0), v_win(1), v_ctx],
        out_specs=pl.BlockSpec((T_Q, D), lambda b, iq, *_: (q_blk(b, iq), 0)),
    )
    return pl.pallas_call(
        functools.partial(_attn_body, n_q_blocks=n_q_blocks),
        grid_spec=grid_spec,
        out_shape=jax.ShapeDtypeStruct((n_out, D), BF16),
        compiler_params=pltpu.CompilerParams(
            dimension_semantics=("arbitrary", "arbitrary"), vmem_limit_bytes=VMEM_BIG),
        name="attn_core",
    )(sink.astype(F32), qt, k, k, k, k, vt, vt, vt, vt)


def kernel(x, c, ctx, c_ctx, ada_w, ada_b, norm_mix_g, norm_ffn_g, ev_w_in, ev_ln_g, ev_ln_b, ev_ws,
           ev_bs, ev_conv_w, ev_conv_b, ev_cnorm_g, ev_w_out, od_w_qkv, od_q_g, od_k_g, od_sink, od_w_o,
           ff_w1, ff_w3, ff_w2, moe_router, moe_w1, moe_w3, moe_w2):
    assert x.shape == (NB, SEQ, D) and ctx.shape == (NB, CTX_LEN, D)
    mods = _ada_mods(c, c_ctx, ada_w, ada_b)
    xa = None
    ev_w_in_b, ev_w_out_b = ev_w_in.astype(BF16), ev_w_out.astype(BF16)
    for li in range(DEPTH):
        need_ctx = li < DEPTH - 1
        j = li // 2
        sh1, sc1, g1, sh2, sc2, g2 = mods[li]
        if li % 2 == 0:
            xin = (x.reshape(N_LAT, D), ctx.reshape(N_CTX, D)) if li == 0 else (xa, None)
            xa = _even_mixer(*xin, sh1, sc1, g1, norm_mix_g[li], ev_w_in_b, ev_ln_g[j], ev_ln_b[j],
                             ev_ws[j], ev_bs[j], ev_conv_w[j], ev_conv_b[j], ev_cnorm_g[j], ev_w_out_b, j)
            xa = _ffn_dense(xa, sh2, sc2, g2, norm_ffn_g[li], ff_w1, ff_w3, ff_w2, j)
        else:
            qt, k, vt = _qkv(xa, sh1, sc1, norm_mix_g[li], od_w_qkv[j], od_q_g[j], od_k_g[j])
            o = _attention(qt, k, vt, od_sink[j], need_ctx)
            xa = _attn_out_moe(o, od_w_o[j], g1, xa, sh2, sc2, g2, norm_ffn_g[li], moe_router[j],
                               moe_w1, moe_w3, moe_w2, j)
    return xa[:N_LAT].reshape(NB, SEQ, D)
```
